```python
import jax
import jax.numpy as jnp
from jax import lax
import numpy as np

D_MODEL = 2048
BATCH = 4
SEQ = 4096
DEPTH = 4

N_MIXERS = 3
NORM_EPS = 1e-6

MOBA_HEADS = 16
MOBA_HEAD_DIM = D_MODEL // MOBA_HEADS
MOBA_BLOCK = 256
MOBA_TOPK = 3
MOBA_QCHUNK = 16
ROPE_THETA = 500000.0
ROT_DIM = MOBA_HEAD_DIM // 4

RWKV_HEAD_SIZE = 64
RWKV_HEADS = D_MODEL // RWKV_HEAD_SIZE
RWKV_DECAY_LORA = max(32, int(round(1.8 * D_MODEL ** 0.5 / 32)) * 32)
RWKV_AAA_LORA = max(32, int(round(1.8 * D_MODEL ** 0.5 / 32)) * 32)
RWKV_GATE_LORA = max(32, int(round(0.6 * D_MODEL ** 0.8 / 32)) * 32)
RWKV_GN_EPS = 64e-5

MLSTM_HEADS = 8
MLSTM_DV = D_MODEL // MLSTM_HEADS
MLSTM_DQK = MLSTM_DV // 2
MLSTM_CHUNK = 64
GATE_SOFTCAP = 15.0

D_FF = ((8 * D_MODEL // 3 + 255) // 256) * 256
CONV_WIDTH = 3

N_MOBA_LAYERS = len(range(0, DEPTH, N_MIXERS))
N_RWKV_LAYERS = len(range(1, DEPTH, N_MIXERS))
N_MLSTM_LAYERS = len(range(2, DEPTH, N_MIXERS))

kernel_name = 'hybrid_moba_rwkv7_mlstm_convffn'


def rms_norm(x, gain, eps=NORM_EPS):
    xf = x.astype(jnp.float32)
    y = xf * lax.rsqrt(jnp.mean(xf * xf, axis=-1, keepdims=True) + eps)
    return (y * gain.astype(jnp.float32)).astype(x.dtype)


def rope_tables(seq_len):
    inv_freq = jnp.float32(ROPE_THETA) ** (-jnp.arange(0, ROT_DIM, 2, dtype=jnp.float32) / ROT_DIM)
    ang = jnp.arange(seq_len, dtype=jnp.float32)[:, None] * inv_freq[None, :]
    return jnp.cos(ang), jnp.sin(ang)


def apply_partial_rope(x, cos, sin):
    half = ROT_DIM // 2
    x1 = x[..., :half].astype(jnp.float32)
    x2 = x[..., half:ROT_DIM].astype(jnp.float32)
    r1 = (x1 * cos - x2 * sin).astype(x.dtype)
    r2 = (x2 * cos + x1 * sin).astype(x.dtype)
    return jnp.concatenate([r1, r2, x[..., ROT_DIM:]], axis=-1)


def moba_attention(h, wqkv, wo, q_gain, k_gain, cos, sin):
    B, S, _ = h.shape
    H, Dh, BLK, QC = MOBA_HEADS, MOBA_HEAD_DIM, MOBA_BLOCK, MOBA_QCHUNK
    qkv = (h @ wqkv).reshape(B, S, 3, H, Dh)
    q = jnp.transpose(qkv[:, :, 0], (0, 2, 1, 3))
    k = jnp.transpose(qkv[:, :, 1], (0, 2, 1, 3))
    v = jnp.transpose(qkv[:, :, 2], (0, 2, 1, 3))
    q = apply_partial_rope(rms_norm(q, q_gain), cos, sin)
    k = apply_partial_rope(rms_norm(k, k_gain), cos, sin)
    n_blocks = -(-S // BLK)
    s_pad = n_blocks * BLK
    pad = ((0, 0), (0, 0), (0, s_pad - S), (0, 0))
    q, k, v = jnp.pad(q, pad), jnp.pad(k, pad), jnp.pad(v, pad)
    scale = Dh ** -0.5
    k_blocks = k.reshape(B, H, n_blocks, BLK, Dh)
    v_blocks = v.reshape(B, H, n_blocks, BLK, Dh)
    n_sel = min(MOBA_TOPK, n_blocks - 1)
    if n_sel > 0:
        k_mean = jnp.mean(k_blocks.astype(jnp.float32), axis=3)
        gate = jnp.einsum('bhsd,bhnd->bhsn', q.astype(jnp.float32), k_mean)
        q_block = jnp.arange(s_pad) // BLK
        fully_past = jnp.arange(n_blocks)[None, :] < q_block[:, None]
        gate = jnp.where(fully_past, gate, -jnp.inf)
        _, sel = lax.top_k(gate, n_sel)
    else:
        sel = jnp.zeros((B, H, s_pad, 0), jnp.int32)
    n_chunks = s_pad // QC
    q_chunks = jnp.transpose(q.reshape(B, H, n_chunks, QC, Dh), (2, 0, 1, 3, 4))
    sel_chunks = jnp.transpose(sel.reshape(B, H, n_chunks, QC, n_sel), (2, 0, 1, 3, 4))
    starts = jnp.arange(n_chunks, dtype=jnp.int32) * QC
    b_idx = jnp.arange(B)[:, None, None, None]
    h_idx = jnp.arange(H)[None, :, None, None]

    def chunk_attn(args):
        q_c, sel_c, start = args
        own = start // BLK
        q_pos = start + jnp.arange(QC)
        k_own = lax.dynamic_slice_in_dim(k, own * BLK, BLK, axis=2)
        v_own = lax.dynamic_slice_in_dim(v, own * BLK, BLK, axis=2)
        k_pos = own * BLK + jnp.arange(BLK)
        s_own = jnp.einsum('bhqd,bhkd->bhqk', q_c, k_own).astype(jnp.float32) * scale
        s_own = jnp.where(k_pos[None, :] <= q_pos[:, None], s_own, -jnp.inf)
        k_sel = k_blocks[b_idx, h_idx, sel_c]
        v_sel = v_blocks[b_idx, h_idx, sel_c]
        s_sel = jnp.einsum('bhqd,bhqnkd->bhqnk', q_c, k_sel).astype(jnp.float32) * scale
        slot_ok = jnp.arange(n_sel) < own
        s_sel = jnp.where(slot_ok[:, None], s_sel, -jnp.inf)
        s_all = jnp.concatenate([s_sel.reshape(B, H, QC, n_sel * BLK), s_own], axis=-1)
        p = jax.nn.softmax(s_all, axis=-1).astype(v.dtype)
        p_sel = p[..., :n_sel * BLK].reshape(B, H, QC, n_sel, BLK)
        p_own = p[..., n_sel * BLK:]
        return (jnp.einsum('bhqnk,bhqnkd->bhqd', p_sel, v_sel)
                + jnp.einsum('bhqk,bhkd->bhqd', p_own, v_own))

    out = lax.map(chunk_attn, (q_chunks, sel_chunks, starts))
    out = jnp.transpose(out, (1, 0, 3, 2, 4)).reshape(B, s_pad, H * Dh)[:, :S]
    return out @ wo


def rwkv7_time_mix(h, mu, w_r, w_k, w_v, w_o, w0, w1, w2, a0, a1, a2, g1, g2,
                   k_k, k_a, r_k, lnx_w, lnx_b):
    B, S, D = h.shape
    H, N = RWKV_HEADS, RWKV_HEAD_SIZE
    f32 = jnp.float32
    xx = jnp.pad(h, ((0, 0), (1, 0), (0, 0)))[:, :S] - h
    x_r = h + xx * mu[0]
    x_w = h + xx * mu[1]
    x_k = h + xx * mu[2]
    x_v = h + xx * mu[3]
    x_a = h + xx * mu[4]
    x_g = h + xx * mu[5]
    r = x_r @ w_r
    k = x_k @ w_k
    v = x_v @ w_v
    w_log = -jax.nn.softplus(-(w0 + jnp.tanh(x_w @ w1) @ w2).astype(f32)) - 0.5
    decay = jnp.exp(-jnp.exp(w_log))
    a = jax.nn.sigmoid((a0 + (x_a @ a1) @ a2).astype(f32))
    g = jax.nn.sigmoid(x_g @ g1) @ g2
    kk = (k * k_k).astype(f32).reshape(B, S, H, N)
    kk = kk / jnp.maximum(jnp.sqrt(jnp.sum(kk * kk, axis=-1, keepdims=True)), 1e-12)
    kk = kk.reshape(B, S, D)
    k_mod = k.astype(f32) * (1.0 + (a - 1.0) * k_a)
    r32, v32 = r.astype(f32), v.astype(f32)

    def heads_tm(t):
        return jnp.transpose(t.reshape(B, S, H, N), (1, 0, 2, 3))

    xs = (heads_tm(r32), heads_tm(decay), heads_tm(k_mod), heads_tm(v32),
          heads_tm(-kk), heads_tm(kk * a))

    def step(state, inp):
        r_t, w_t, k_t, v_t, a_t, b_t = inp
        sa = jnp.einsum('bhvk,bhk->bhv', state, a_t)
        state = (state * w_t[:, :, None, :] + sa[..., None] * b_t[:, :, None, :]
                 + v_t[..., None] * k_t[:, :, None, :])
        return state, jnp.einsum('bhvk,bhk->bhv', state, r_t)

    _, y = lax.scan(step, jnp.zeros((B, H, N, N), f32), xs)
    y = jnp.transpose(y, (1, 0, 2, 3))
    mean = jnp.mean(y, axis=-1, keepdims=True)
    var = jnp.mean(jnp.square(y - mean), axis=-1, keepdims=True)
    y = ((y - mean) * lax.rsqrt(var + RWKV_GN_EPS)).reshape(B, S, D)
    y = y * lnx_w.astype(f32) + lnx_b.astype(f32)
    bonus = jnp.sum(r32.reshape(B, S, H, N) * k_mod.reshape(B, S, H, N) * r_k.astype(f32),
                    axis=-1, keepdims=True) * v32.reshape(B, S, H, N)
    out = (y + bonus.reshape(B, S, D)) * g.astype(f32)
    return out.astype(h.dtype) @ w_o


def mlstm_mixer(h, w_in, b_if, head_gain, w_out):
    B, S, _ = h.shape
    H, Dk, Dv, L = MLSTM_HEADS, MLSTM_DQK, MLSTM_DV, MLSTM_CHUNK
    f32 = jnp.float32
    proj = h @ w_in
    o0, o1, o2, o3, o4 = H * Dk, 2 * H * Dk, 2 * H * Dk + H * Dv, 2 * H * Dk + 2 * H * Dv, 2 * H * Dk + 2 * H * Dv + H
    q, k, v, o = proj[..., :o0], proj[..., o0:o1], proj[..., o1:o2], proj[..., o2:o3]
    i_pre, f_pre = proj[..., o3:o4], proj[..., o4:]

    def heads(t, d):
        return jnp.transpose(t.reshape(B, S, H, d), (0, 2, 1, 3)).astype(f32)

    q = heads(q, Dk)
    k = heads(k, Dk) * (Dk ** -0.5)
    v = heads(v, Dv)
    cap = lambda z: GATE_SOFTCAP * jnp.tanh(z / GATE_SOFTCAP)
    log_i = jnp.transpose(cap(i_pre.astype(f32) + b_if[0].astype(f32)), (0, 2, 1))
    log_f = jnp.transpose(jax.nn.log_sigmoid(cap(f_pre.astype(f32) + b_if[1].astype(f32))), (0, 2, 1))
    nc = S // L

    def chunk(t):
        return jnp.moveaxis(t.reshape(B, H, nc, L, *t.shape[3:]), 2, 0)

    causal = jnp.tril(jnp.ones((L, L), dtype=bool))

    def step(carry, inp):
        C, n, m = carry
        q_c, k_c, v_c, li, lf = inp
        b = jnp.cumsum(lf, axis=-1)
        dmat = jnp.where(causal, b[..., :, None] - b[..., None, :] + li[..., None, :], -jnp.inf)
        inter = b + m[..., None]
        m_t = jnp.maximum(inter, jnp.max(dmat, axis=-1))
        s = jnp.einsum('bhtd,bhsd->bhts', q_c, k_c) * jnp.exp(dmat - m_t[..., None])
        w_inter = jnp.exp(inter - m_t)
        num = (jnp.einsum('bhts,bhsv->bhtv', s, v_c)
               + w_inter[..., None] * jnp.einsum('bhvd,bhtd->bhtv', C, q_c))
        den = jnp.sum(s, axis=-1) + w_inter * jnp.einsum('bhd,bhtd->bht', n, q_c)
        h_c = num / jnp.maximum(jnp.abs(den), jnp.exp(-m_t))[..., None]
        b_last = b[..., -1]
        d_s = b_last[..., None] - b + li
        m_new = jnp.maximum(b_last + m, jnp.max(d_s, axis=-1))
        w_s = jnp.exp(d_s - m_new[..., None])
        w_c = jnp.exp(b_last + m - m_new)
        C_new = w_c[..., None, None] * C + jnp.einsum('bhs,bhsv,bhsd->bhvd', w_s, v_c, k_c)
        n_new = w_c[..., None] * n + jnp.einsum('bhs,bhsd->bhd', w_s, k_c)
        return (C_new, n_new, m_new), h_c

    carry0 = (jnp.zeros((B, H, Dv, Dk), f32), jnp.zeros((B, H, Dk), f32), jnp.zeros((B, H), f32))
    _, hs = lax.scan(step, carry0, (chunk(q), chunk(k), chunk(v), chunk(log_i), chunk(log_f)))
    hs = jnp.moveaxis(hs, 0, 2).reshape(B, H, S, Dv)
    hs = jnp.transpose(hs, (0, 2, 1, 3))
    hs = rms_norm(hs, head_gain.reshape(H, Dv)).reshape(B, S, H * Dv)
    out = hs * jax.nn.sigmoid(o.astype(f32))
    return out.astype(h.dtype) @ w_out


def conv_ffn(h, w_up, conv_w, conv_b, w_down):
    u = h @ w_up
    u = lax.conv_general_dilated(u, conv_w[:, None, :], window_strides=(1,),
                                 padding=[(CONV_WIDTH - 1, 0)],
                                 dimension_numbers=('NWC', 'WIO', 'NWC'),
                                 feature_group_count=u.shape[-1]) + conv_b
    gate, up = u[..., :D_FF], u[..., D_FF:]
    return (jax.nn.silu(gate) * up) @ w_down


def setup_inputs(seed: int = 0) -> dict:
    key = jax.random.key(seed)
    ks = iter(jax.random.split(key, 40))

    def nrm(shape, scale):
        return jax.random.normal(next(ks), shape, jnp.float32) * scale

    def gain(shape):
        return 1.0 + nrm(shape, 0.02)

    D, F = D_MODEL, D_FF
    nA, nB, nC = N_MOBA_LAYERS, N_RWKV_LAYERS, N_MLSTM_LAYERS
    d_in_mlstm = MLSTM_HEADS * (2 * MLSTM_DQK + 2 * MLSTM_DV + 2)
    return {
        'x': nrm((BATCH, SEQ, D), 1.0),
        'moba_norm': gain((nA, D)),
        'moba_wqkv': nrm((nA, D, 3 * D), D ** -0.5),
        'moba_q_gain': gain((nA, MOBA_HEAD_DIM)),
        'moba_k_gain': gain((nA, MOBA_HEAD_DIM)),
        'moba_wo': nrm((nA, D, D), D ** -0.5),
        'rwkv_norm': gain((nB, D)),
        'rwkv_mu': jax.random.uniform(next(ks), (nB, 6, D), jnp.float32),
        'rwkv_w_r': nrm((nB, D, D), D ** -0.5),
        'rwkv_w_k': nrm((nB, D, D), D ** -0.5),
        'rwkv_w_v': nrm((nB, D, D), D ** -0.5),
        'rwkv_w_o': nrm((nB, D, D), D ** -0.5),
        'rwkv_w0': nrm((nB, D), 0.5),
        'rwkv_w1': nrm((nB, D, RWKV_DECAY_LORA), D ** -0.5),
        'rwkv_w2': nrm((nB, RWKV_DECAY_LORA, D), RWKV_DECAY_LORA ** -0.5),
        'rwkv_a0': nrm((nB, D), 0.1),
        'rwkv_a1': nrm((nB, D, RWKV_AAA_LORA), D ** -0.5),
        'rwkv_a2': nrm((nB, RWKV_AAA_LORA, D), RWKV_AAA_LORA ** -0.5),
        'rwkv_g1': nrm((nB, D, RWKV_GATE_LORA), D ** -0.5),
        'rwkv_g2': nrm((nB, RWKV_GATE_LORA, D), RWKV_GATE_LORA ** -0.5),
        'rwkv_k_k': 0.85 + nrm((nB, D), 0.05),
        'rwkv_k_a': 1.0 + nrm((nB, D), 0.05),
        'rwkv_r_k': nrm((nB, RWKV_HEADS, RWKV_HEAD_SIZE), 0.1),
        'rwkv_lnx_w': gain((nB, D)),
        'rwkv_lnx_b': nrm((nB, D), 0.02),
        'mlstm_norm': gain((nC, D)),
        'mlstm_w_in': nrm((nC, D, d_in_mlstm), D ** -0.5),
        'mlstm_b_if': nrm((nC, 2, MLSTM_HEADS), 0.1) + jnp.array([0.0, 3.0], jnp.float32)[None, :, None],
        'mlstm_head_gain': gain((nC, MLSTM_HEADS * MLSTM_DV)),
        'mlstm_w_out': nrm((nC, MLSTM_HEADS * MLSTM_DV, D), (MLSTM_HEADS * MLSTM_DV) ** -0.5),
        'ffn_norm': gain((DEPTH, D)),
        'ffn_w_up': nrm((DEPTH, D, 2 * F), D ** -0.5),
        'ffn_conv_w': nrm((DEPTH, CONV_WIDTH, 2 * F), CONV_WIDTH ** -0.5),
        'ffn_conv_b': nrm((DEPTH, 2 * F), 0.02),
        'ffn_w_down': nrm((DEPTH, F, D), F ** -0.5),
    }


def reference(x, moba_norm, moba_wqkv, moba_q_gain, moba_k_gain, moba_wo,
              rwkv_norm, rwkv_mu, rwkv_w_r, rwkv_w_k, rwkv_w_v, rwkv_w_o,
              rwkv_w0, rwkv_w1, rwkv_w2, rwkv_a0, rwkv_a1, rwkv_a2, rwkv_g1, rwkv_g2,
              rwkv_k_k, rwkv_k_a, rwkv_r_k, rwkv_lnx_w, rwkv_lnx_b,
              mlstm_norm, mlstm_w_in, mlstm_b_if, mlstm_head_gain, mlstm_w_out,
              ffn_norm, ffn_w_up, ffn_conv_w, ffn_conv_b, ffn_w_down):
    cos, sin = rope_tables(x.shape[1])
    for i in range(DEPTH):
        kind, j = i % N_MIXERS, i // N_MIXERS
        if kind == 0:
            x = x + moba_attention(rms_norm(x, moba_norm[j]), moba_wqkv[j], moba_wo[j],
                                   moba_q_gain[j], moba_k_gain[j], cos, sin)
        elif kind == 1:
            x = x + rwkv7_time_mix(rms_norm(x, rwkv_norm[j]), rwkv_mu[j], rwkv_w_r[j], rwkv_w_k[j],
                                   rwkv_w_v[j], rwkv_w_o[j], rwkv_w0[j], rwkv_w1[j], rwkv_w2[j],
                                   rwkv_a0[j], rwkv_a1[j], rwkv_a2[j], rwkv_g1[j], rwkv_g2[j],
                                   rwkv_k_k[j], rwkv_k_a[j], rwkv_r_k[j], rwkv_lnx_w[j], rwkv_lnx_b[j])
        else:
            x = x + mlstm_mixer(rms_norm(x, mlstm_norm[j]), mlstm_w_in[j], mlstm_b_if[j],
                                mlstm_head_gain[j], mlstm_w_out[j])
        x = x + conv_ffn(rms_norm(x, ffn_norm[i]), ffn_w_up[i], ffn_conv_w[i], ffn_conv_b[i], ffn_w_down[i])
    return x
```

```python
import functools

import jax
import jax.numpy as jnp
from jax import lax
from jax.experimental import pallas as pl
from jax.experimental.pallas import tpu as pltpu

F32 = jnp.float32
BF16 = jnp.bfloat16

LANES = 128
SUBLANES = 8
VMEM_LIMIT_BYTES = 56 * 1024 * 1024

NORM_EPS = 1e-6
MOBA_HEADS = 16
MOBA_HEAD_DIM = 128
MOBA_BLOCK = 256
MOBA_TOPK = 3
ROPE_THETA = 500000.0
ROT_DIM = MOBA_HEAD_DIM // 4
RWKV_HEAD_SIZE = 64
RWKV_GN_EPS = 64e-5
MLSTM_HEADS = 8
MLSTM_DV = 256
MLSTM_DQK = 128
MLSTM_CHUNK = 64
GATE_SOFTCAP = 15.0
CONV_WIDTH = 3
NEG_BIG = -1e30

HI = lax.Precision.HIGHEST


def _params(*semantics):
    return pltpu.CompilerParams(dimension_semantics=semantics, vmem_limit_bytes=VMEM_LIMIT_BYTES)


def _pad_cols(w, n):
    return jnp.pad(w, ((0, 0), (0, n - w.shape[1])))


def _pad_rows(w, n):
    return jnp.pad(w, ((0, n - w.shape[0]), (0, 0)))


def _rmsnorm_kernel(x_ref, g_ref, o_ref):
    x = x_ref[...]
    y = x * lax.rsqrt(jnp.mean(x * x, axis=-1, keepdims=True) + NORM_EPS)
    o_ref[...] = (y * g_ref[...]).astype(o_ref.dtype)


def rmsnorm(x, gain, tm=512):
    m, d = x.shape
    return pl.pallas_call(
        _rmsnorm_kernel,
        grid=(m // tm,),
        in_specs=[pl.BlockSpec((tm, d), lambda i: (i, 0)), pl.BlockSpec((1, d), lambda i: (0, 0))],
        out_specs=pl.BlockSpec((tm, d), lambda i: (i, 0)),
        out_shape=jax.ShapeDtypeStruct((m, d), BF16),
        compiler_params=_params("parallel"),
        name="rmsnorm",
    )(x, gain.reshape(1, d))


def _mm_kernel(*refs, act, has_bias, has_res):
    x_ref, w_ref = refs[0], refs[1]
    o_ref = refs[-1]
    acc = jnp.dot(x_ref[...], w_ref[...], preferred_element_type=F32)
    k = 2
    if has_bias:
        acc = acc + refs[k][...]
        k += 1
    if act == "tanh":
        acc = jnp.tanh(acc)
    elif act == "sigmoid":
        acc = jax.nn.sigmoid(acc)
    if has_res:
        acc = acc + refs[k][...]
    o_ref[...] = acc.astype(o_ref.dtype)


def matmul(x, w, *, bias=None, act=None, residual=None, out_dtype=F32, tm=1024, tn=1024):
    m, kdim = x.shape
    n = w.shape[1]
    tm, tn = min(tm, m), min(tn, n)
    assert m % tm == 0 and n % tn == 0, (m, n, tm, tn)
    in_specs = [pl.BlockSpec((tm, kdim), lambda j, i: (i, 0)), pl.BlockSpec((kdim, tn), lambda j, i: (0, j))]
    args = [x, w]
    if bias is not None:
        in_specs.append(pl.BlockSpec((1, tn), lambda j, i: (0, j)))
        args.append(bias.reshape(1, n))
    if residual is not None:
        in_specs.append(pl.BlockSpec((tm, tn), lambda j, i: (i, j)))
        args.append(residual)
    return pl.pallas_call(
        functools.partial(_mm_kernel, act=act, has_bias=bias is not None, has_res=residual is not None),
        grid=(n // tn, m // tm),
        in_specs=in_specs,
        out_specs=pl.BlockSpec((tm, tn), lambda j, i: (i, j)),
        out_shape=jax.ShapeDtypeStruct((m, n), out_dtype),
        compiler_params=_params("parallel", "parallel"),
        name="matmul",
    )(*args)


def _ffn_up_kernel(x_ref, h_ref, wg_ref, wu_ref, cwg_ref, cwu_ref, cbg_ref, cbu_ref, o_ref, *, tiles_per_seq):
    i = pl.program_id(1)
    first = (i % tiles_per_seq) == 0
    x = x_ref[...]
    halo = h_ref[...]
    tm = x.shape[0]
    tn = o_ref.shape[1]
    row = lax.broadcasted_iota(jnp.int32, (tm, tn), 0)

    def causal_conv(w_ref, cw_ref, cb_ref):
        u = jnp.dot(x, w_ref[...], preferred_element_type=F32)
        uh = jnp.dot(halo, w_ref[...], preferred_element_type=F32)
        uh = jnp.where(first, 0.0, uh)
        u1 = jnp.where(row == 0, uh[7:8, :], pltpu.roll(u, 1, 0))
        u2 = pltpu.roll(u, 2, 0)
        u2 = jnp.where(row == 0, uh[6:7, :], jnp.where(row == 1, uh[7:8, :], u2))
        cw = cw_ref[...]
        return u2 * cw[0:1, :] + u1 * cw[1:2, :] + u * cw[2:3, :] + cb_ref[...]

    gate = causal_conv(wg_ref, cwg_ref, cbg_ref)
    up = causal_conv(wu_ref, cwu_ref, cbu_ref)
    o_ref[...] = (gate * jax.nn.sigmoid(gate) * up).astype(o_ref.dtype)


def ffn_up(h, w_up, conv_w, conv_b, seq, tm=1024, tn=512):
    m, d = h.shape
    f = w_up.shape[1] // 2
    nf = f // tn
    assert f % tn == 0 and seq % tm == 0
    hb = tm // SUBLANES
    return pl.pallas_call(
        functools.partial(_ffn_up_kernel, tiles_per_seq=seq // tm),
        grid=(nf, m // tm),
        in_specs=[
            pl.BlockSpec((tm, d), lambda c, i: (i, 0)),
            pl.BlockSpec((SUBLANES, d), lambda c, i: (jnp.maximum(i * hb - 1, 0), 0)),
            pl.BlockSpec((d, tn), lambda c, i: (0, c)),
            pl.BlockSpec((d, tn), lambda c, i: (0, c + nf)),
            pl.BlockSpec((CONV_WIDTH, tn), lambda c, i: (0, c)),
            pl.BlockSpec((CONV_WIDTH, tn), lambda c, i: (0, c + nf)),
            pl.BlockSpec((1, tn), lambda c, i: (0, c)),
            pl.BlockSpec((1, tn), lambda c, i: (0, c + nf)),
        ],
        out_specs=pl.BlockSpec((tm, tn), lambda c, i: (i, c)),
        out_shape=jax.ShapeDtypeStruct((m, f), BF16),
        compiler_params=_params("parallel", "parallel"),
        name="ffn_up_conv",
    )(h, h, w_up, w_up, conv_w, conv_w, conv_b.reshape(1, 2 * f), conv_b.reshape(1, 2 * f))


def conv_ffn(x, norm_g, w_up, conv_w, conv_b, w_down, seq):
    h = rmsnorm(x, norm_g)
    act = ffn_up(h, w_up.astype(BF16), conv_w, conv_b, seq)
    return matmul(act, w_down.astype(BF16), residual=x, tm=512, tn=512)


def _rope_tables(seq):
    half = ROT_DIM // 2
    inv_freq = jnp.float32(ROPE_THETA) ** (-jnp.arange(0, ROT_DIM, 2, dtype=F32) / ROT_DIM)
    ang = jnp.arange(seq, dtype=F32)[:, None] * inv_freq[None, :]
    cos, sin = jnp.cos(ang), jnp.sin(ang)
    ones = jnp.ones((seq, MOBA_HEAD_DIM - ROT_DIM), F32)
    zeros_h = jnp.zeros((seq, half), F32)
    zeros_r = jnp.zeros((seq, MOBA_HEAD_DIM - ROT_DIM), F32)
    c_tab = jnp.concatenate([cos, cos, ones], axis=1)
    s_lo = jnp.concatenate([-sin, zeros_h, zeros_r], axis=1)
    s_hi = jnp.concatenate([zeros_h, sin, zeros_r], axis=1)
    return c_tab, s_lo, s_hi


def _moba_prep_kernel(qkv_ref, c_ref, slo_ref, shi_ref, qg_ref, kg_ref,
                      q_ref, k_ref, vt_ref, sel_ref, kmean_scr, *, blocks_per_seq):
    n = pl.program_id(0) % blocks_per_seq
    d = MOBA_HEADS * MOBA_HEAD_DIM
    half = ROT_DIM // 2
    c_tab, s_lo, s_hi = c_ref[...], slo_ref[...], shi_ref[...]

    @pl.when(n == 0)
    def _():
        kmean_scr[...] = jnp.zeros_like(kmean_scr)

    def norm_rope(x, gain):
        y = x * lax.rsqrt(jnp.mean(x * x, axis=-1, keepdims=True) + NORM_EPS) * gain
        return (y * c_tab + pltpu.roll(y, MOBA_HEAD_DIM - half, 1) * s_lo + pltpu.roll(y, half, 1) * s_hi)

    nb = sel_ref.shape[1]
    blk = lax.broadcasted_iota(jnp.int32, (nb, MOBA_BLOCK), 0)
    eligible = blk < n
    for h in range(MOBA_HEADS):
        lo = h * MOBA_HEAD_DIM
        q = norm_rope(qkv_ref[:, lo:lo + MOBA_HEAD_DIM], qg_ref[...])
        k = norm_rope(qkv_ref[:, d + lo:d + lo + MOBA_HEAD_DIM], kg_ref[...])
        q_ref[:, lo:lo + MOBA_HEAD_DIM] = q.astype(q_ref.dtype)
        k_ref[:, lo:lo + MOBA_HEAD_DIM] = k.astype(k_ref.dtype)
        vt_ref[lo:lo + MOBA_HEAD_DIM, :] = qkv_ref[:, 2 * d + lo:2 * d + lo + MOBA_HEAD_DIM].T.astype(vt_ref.dtype)
        gate = lax.dot_general(kmean_scr[:, lo:lo + MOBA_HEAD_DIM], q, (((1,), (1,)), ((), ())),
                               precision=HI, preferred_element_type=F32)
        gate = jnp.where(eligible, gate, -jnp.inf)
        sel = jnp.zeros((nb, MOBA_BLOCK), F32)
        for j in range(nb):
            gj = gate[j:j + 1, :]
            beats = jnp.where(gate > gj, 1.0, jnp.where((gate == gj) & (blk < j), 1.0, 0.0))
            rank = jnp.sum(beats, axis=0, keepdims=True)
            chosen = jnp.where((rank < MOBA_TOPK) & (j < n), 1.0, 0.0)
            sel = jnp.where(blk == j, chosen, sel)
        sel_ref[h] = sel
        km_rows = lax.broadcasted_iota(jnp.int32, (nb, MOBA_HEAD_DIM), 0)
        kmean_scr[:, lo:lo + MOBA_HEAD_DIM] = jnp.where(km_rows == n, jnp.mean(k, axis=0, keepdims=True),
                                                        kmean_scr[:, lo:lo + MOBA_HEAD_DIM])


def _moba_attn_kernel(q_ref, k_ref, vt_ref, sel_ref, o_ref):
    i = pl.program_id(2)
    scale = MOBA_HEAD_DIM ** -0.5
    q = q_ref[...]

    def scores_t(kj):
        return lax.dot_general(kj, q, (((1,), (1,)), ((), ())), preferred_element_type=F32) * scale

    own = pl.multiple_of(i * MOBA_BLOCK, MOBA_BLOCK)
    s_t = scores_t(k_ref[pl.ds(own, MOBA_BLOCK), :])
    kpos = lax.broadcasted_iota(jnp.int32, s_t.shape, 0)
    qpos = lax.broadcasted_iota(jnp.int32, s_t.shape, 1)
    s_t = jnp.where(kpos <= qpos, s_t, NEG_BIG)
    m0 = jnp.max(s_t, axis=0, keepdims=True)
    p = jnp.exp(s_t - m0)
    l0 = jnp.sum(p, axis=0, keepdims=True)
    acc0 = jnp.dot(vt_ref[:, pl.ds(own, MOBA_BLOCK)], p.astype(BF16), preferred_element_type=F32)

    def body(j, carry):
        m_run, l_run, acc = carry
        start = pl.multiple_of(j * MOBA_BLOCK, MOBA_BLOCK)
        s_j = scores_t(k_ref[pl.ds(start, MOBA_BLOCK), :])
        s_j = jnp.where(sel_ref[pl.ds(j, 1), :] > 0.5, s_j, NEG_BIG)
        m_new = jnp.maximum(m_run, jnp.max(s_j, axis=0, keepdims=True))
        alpha = jnp.exp(m_run - m_new)
        p_j = jnp.exp(s_j - m_new)
        l_new = alpha * l_run + jnp.sum(p_j, axis=0, keepdims=True)
        acc = alpha * acc + jnp.dot(vt_ref[:, pl.ds(start, MOBA_BLOCK)], p_j.astype(BF16),
                                    preferred_element_type=F32)
        return m_new, l_new, acc

    _, l_fin, acc = lax.fori_loop(0, i, body, (m0, l0, acc0))
    o_ref[...] = (acc / l_fin).T.astype(o_ref.dtype)


def moba_layer(x, norm_g, wqkv, q_gain, k_gain, wo, batch, seq):
    m, d = x.shape
    nb = seq // MOBA_BLOCK
    h = rmsnorm(x, norm_g)
    qkv = matmul(h, wqkv.astype(BF16))
    c_tab, s_lo, s_hi = _rope_tables(seq)
    tab_spec = pl.BlockSpec((MOBA_BLOCK, MOBA_HEAD_DIM), lambda i: (i % nb, 0))
    gain_spec = pl.BlockSpec((1, MOBA_HEAD_DIM), lambda i: (0, 0))
    q, k, vt, sel = pl.pallas_call(
        functools.partial(_moba_prep_kernel, blocks_per_seq=nb),
        grid=(m // MOBA_BLOCK,),
        in_specs=[pl.BlockSpec((MOBA_BLOCK, 3 * d), lambda i: (i, 0)), tab_spec, tab_spec, tab_spec,
                  gain_spec, gain_spec],
        out_specs=[pl.BlockSpec((MOBA_BLOCK, d), lambda i: (i, 0)),
                   pl.BlockSpec((MOBA_BLOCK, d), lambda i: (i, 0)),
                   pl.BlockSpec((d, MOBA_BLOCK), lambda i: (0, i)),
                   pl.BlockSpec((None, MOBA_HEADS, nb, MOBA_BLOCK), lambda i: (i, 0, 0, 0))],
        out_shape=[jax.ShapeDtypeStruct((m, d), BF16), jax.ShapeDtypeStruct((m, d), BF16),
                   jax.ShapeDtypeStruct((d, m), BF16),
                   jax.ShapeDtypeStruct((m // MOBA_BLOCK, MOBA_HEADS, nb, MOBA_BLOCK), F32)],
        scratch_shapes=[pltpu.VMEM((nb, d), F32)],
        compiler_params=_params("arbitrary"),
        name="moba_prep",
    )(qkv, c_tab, s_lo, s_hi, q_gain.reshape(1, -1), k_gain.reshape(1, -1))
    attn = pl.pallas_call(
        _moba_attn_kernel,
        grid=(batch, MOBA_HEADS, nb),
        in_specs=[pl.BlockSpec((MOBA_BLOCK, MOBA_HEAD_DIM), lambda b, hh, i: (b * nb + i, hh)),
                  pl.BlockSpec((seq, MOBA_HEAD_DIM), lambda b, hh, i: (b, hh)),
                  pl.BlockSpec((MOBA_HEAD_DIM, seq), lambda b, hh, i: (hh, b)),
                  pl.BlockSpec((None, None, nb, MOBA_BLOCK), lambda b, hh, i: (b * nb + i, hh, 0, 0))],
        out_specs=pl.BlockSpec((MOBA_BLOCK, MOBA_HEAD_DIM), lambda b, hh, i: (b * nb + i, hh)),
        out_shape=jax.ShapeDtypeStruct((m, d), BF16),
        compiler_params=_params("parallel", "parallel", "parallel"),
        name="moba_attn",
    )(q, k, vt, sel)
    return matmul(attn, wo.astype(BF16), residual=x)


def _head_sum_matrix():
    r = lax.broadcasted_iota(jnp.int32, (LANES, LANES), 0) // RWKV_HEAD_SIZE
    c = lax.broadcasted_iota(jnp.int32, (LANES, LANES), 1) // RWKV_HEAD_SIZE
    return jnp.where(r == c, 1.0, 0.0).astype(F32)


def _head_sum(x, ones_bd):
    return jnp.dot(x, ones_bd, precision=HI, preferred_element_type=F32)


def _rwkv_mix_kernel(x_ref, h_ref, g_ref, mu_ref, *o_refs, tiles_per_seq):
    first = (pl.program_id(0) % tiles_per_seq) == 0

    def norm(x):
        return (x * lax.rsqrt(jnp.mean(x * x, axis=-1, keepdims=True) + NORM_EPS) * g_ref[...])

    h = norm(x_ref[...])
    prev_row = jnp.where(first, 0.0, norm(h_ref[...])[SUBLANES - 1:SUBLANES, :])
    row = lax.broadcasted_iota(jnp.int32, h.shape, 0)
    xx = jnp.where(row == 0, prev_row, pltpu.roll(h, 1, 0)) - h
    for idx, o_ref in enumerate(o_refs):
        o_ref[...] = (h + xx * mu_ref[idx:idx + 1, :]).astype(o_ref.dtype)


def _rwkv_prep_kernel(k_ref, wp_ref, ap_ref, w0_ref, a0_ref, kk_ref, ka_ref,
                      decay_ref, kmod_ref, an_ref, b_ref):
    ones_bd = _head_sum_matrix()
    w_log = -jax.nn.softplus(-(w0_ref[...] + wp_ref[...])) - 0.5
    decay_ref[...] = jnp.exp(-jnp.exp(w_log))
    a = jax.nn.sigmoid(a0_ref[...] + ap_ref[...])
    k = k_ref[...]
    kmod_ref[...] = k * (1.0 + (a - 1.0) * ka_ref[...])
    kk = k * kk_ref[...]
    for c in range(kk.shape[1] // LANES):
        sl = slice(c * LANES, (c + 1) * LANES)
        kc = kk[:, sl]
        nrm = jnp.maximum(jnp.sqrt(_head_sum(kc * kc, ones_bd)), 1e-12)
        kc = kc / nrm
        an_ref[:, sl] = -kc
        b_ref[:, sl] = kc * a[:, sl]


def _rwkv_scan_kernel(r_ref, w_ref, k_ref, v_ref, a_ref, b_ref, y_ref, s_scr, wr_scr, c1_scr, c2_scr, *, pairs):
    tb = r_ref.shape[0]
    n = RWKV_HEAD_SIZE

    @pl.when(pl.program_id(2) == 0)
    def _():
        s_scr[...] = jnp.zeros_like(s_scr)

    ones_bd = _head_sum_matrix()
    r_all = r_ref[...]
    wr_scr[...] = w_ref[...] * r_all
    for g in range(pairs):
        sl = slice(g * LANES, (g + 1) * LANES)
        c1_scr[:, sl] = _head_sum(b_ref[:, sl] * r_all[:, sl], ones_bd)
        c2_scr[:, sl] = _head_sum(k_ref[:, sl] * r_all[:, sl], ones_bd)

    lane = lax.broadcasted_iota(jnp.int32, (n, LANES), 1)
    sub = lax.broadcasted_iota(jnp.int32, (n, LANES), 0)
    lo = lane < n
    diag = (lane % n) == sub

    def seg_bcast(p):
        s_lo = jnp.sum(jnp.where(lo, p, 0.0), axis=1, keepdims=True)
        s_hi = jnp.sum(jnp.where(lo, 0.0, p), axis=1, keepdims=True)
        return jnp.where(lo, s_lo, s_hi)

    sub8 = lax.broadcasted_iota(jnp.int32, (SUBLANES, LANES), 0)

    def step(i, carry):
        t0 = pl.multiple_of(i * SUBLANES, SUBLANES)
        for g in range(pairs):
            sl = slice(g * LANES, (g + 1) * LANES)
            tile = lambda ref: ref[pl.ds(t0, SUBLANES), sl]
            a8, wr8, v8, c18, c28, w8, b8, k8 = (tile(ref) for ref in
                                                  (a_ref, wr_scr, v_ref, c1_scr, c2_scr, w_ref, b_ref, k_ref))
            s = s_scr[g]
            y8 = jnp.zeros((SUBLANES, LANES), F32)
            for j in range(SUBLANES):
                row = lambda x: x[j:j + 1, :]
                sa = seg_bcast(s * row(a8))
                z = seg_bcast(s * row(wr8))
                vb = seg_bcast(jnp.where(diag, row(v8), 0.0))
                yb = z + sa * row(c18) + vb * row(c28)
                y_row = jnp.sum(jnp.where(diag, yb, 0.0), axis=0, keepdims=True)
                y8 = jnp.where(sub8 == j, y_row, y8)
                s = s * row(w8) + sa * row(b8) + vb * row(k8)
            s_scr[g] = s
            y_ref[pl.ds(t0, SUBLANES), sl] = y8
        return carry

    lax.fori_loop(0, tb // SUBLANES, step, 0)


def _rwkv_post_kernel(y_ref, r_ref, kmod_ref, v_ref, g_ref, rk_ref, lw_ref, lb_ref, o_ref):
    ones_bd = _head_sum_matrix()
    inv_n = 1.0 / RWKV_HEAD_SIZE
    for c in range(y_ref.shape[1] // LANES):
        sl = slice(c * LANES, (c + 1) * LANES)
        y = y_ref[:, sl]
        mean = _head_sum(y, ones_bd) * inv_n
        yc = y - mean
        var = _head_sum(yc * yc, ones_bd) * inv_n
        yn = yc * lax.rsqrt(var + RWKV_GN_EPS) * lw_ref[:, sl] + lb_ref[:, sl]
        bonus = _head_sum(r_ref[:, sl] * kmod_ref[:, sl] * rk_ref[:, sl], ones_bd) * v_ref[:, sl]
        o_ref[:, sl] = ((yn + bonus) * g_ref[:, sl]).astype(o_ref.dtype)


def rwkv_layer(x, norm_g, mu, w_r, w_k, w_v, w_o, w0, w1, w2, a0, a1, a2, g1, g2,
               k_k, k_a, r_k, lnx_w, lnx_b, batch, seq):
    m, d = x.shape
    tm = 256
    row_spec = pl.BlockSpec((tm, d), lambda i: (i, 0))
    vec_spec = pl.BlockSpec((1, d), lambda i: (0, 0))
    hb = tm // SUBLANES
    mixed = pl.pallas_call(
        functools.partial(_rwkv_mix_kernel, tiles_per_seq=seq // tm),
        grid=(m // tm,),
        in_specs=[row_spec, pl.BlockSpec((SUBLANES, d), lambda i: (jnp.maximum(i * hb - 1, 0), 0)),
                  vec_spec, pl.BlockSpec((6, d), lambda i: (0, 0))],
        out_specs=[row_spec] * 6,
        out_shape=[jax.ShapeDtypeStruct((m, d), BF16)] * 6,
        compiler_params=_params("parallel"),
        name="rwkv_mix",
    )(x, x, norm_g.reshape(1, d), mu)
    x_r, x_w, x_k, x_v, x_a, x_g = mixed

    def lora_pad(w_in, w_out):
        rank = -(-w_in.shape[1] // LANES) * LANES
        return _pad_cols(w_in, rank).astype(BF16), _pad_rows(w_out, rank).astype(BF16)

    r = matmul(x_r, w_r.astype(BF16))
    k = matmul(x_k, w_k.astype(BF16))
    v = matmul(x_v, w_v.astype(BF16))
    w1p, w2p = lora_pad(w1, w2)
    a1p, a2p = lora_pad(a1, a2)
    g1p, g2p = lora_pad(g1, g2)
    w_pre = matmul(matmul(x_w, w1p, act="tanh", out_dtype=BF16), w2p)
    a_pre = matmul(matmul(x_a, a1p, out_dtype=BF16), a2p)
    gate = matmul(matmul(x_g, g1p, act="sigmoid", out_dtype=BF16), g2p)

    decay, k_mod, a_neg, b_vec = pl.pallas_call(
        _rwkv_prep_kernel,
        grid=(m // tm,),
        in_specs=[row_spec] * 3 + [vec_spec] * 4,
        out_specs=[row_spec] * 4,
        out_shape=[jax.ShapeDtypeStruct((m, d), F32)] * 4,
        compiler_params=_params("parallel"),
        name="rwkv_prep",
    )(k, w_pre, a_pre, w0.reshape(1, d), a0.reshape(1, d), k_k.reshape(1, d), k_a.reshape(1, d))

    pairs, tb = 4, 256
    width = pairs * LANES
    nt = seq // tb
    blk = pl.BlockSpec((tb, width), lambda b, p, t: (b * nt + t, p))
    y = pl.pallas_call(
        functools.partial(_rwkv_scan_kernel, pairs=pairs),
        grid=(batch, d // width, nt),
        in_specs=[blk] * 6,
        out_specs=blk,
        out_shape=jax.ShapeDtypeStruct((m, d), F32),
        scratch_shapes=[pltpu.VMEM((pairs, RWKV_HEAD_SIZE, LANES), F32)] + [pltpu.VMEM((tb, width), F32)] * 3,
        compiler_params=_params("parallel", "parallel", "arbitrary"),
        name="rwkv_scan",
    )(r, decay, k_mod, v, a_neg, b_vec)

    out = pl.pallas_call(
        _rwkv_post_kernel,
        grid=(m // tm,),
        in_specs=[row_spec] * 5 + [vec_spec] * 3,
        out_specs=row_spec,
        out_shape=jax.ShapeDtypeStruct((m, d), BF16),
        compiler_params=_params("parallel"),
        name="rwkv_post",
    )(y, r, k_mod, v, gate, r_k.reshape(1, d), lnx_w.reshape(1, d), lnx_b.reshape(1, d))
    return matmul(out, w_o.astype(BF16), residual=x)


def _softcap(z):
    return GATE_SOFTCAP * jnp.tanh(z / GATE_SOFTCAP)


def _mlstm_kernel(q_ref, k_ref, v_ref, o_ref, gc_ref, gr_ref, hg_ref, out_ref, ct_scr, n_scr, m_scr):
    h = pl.program_id(1)
    L = MLSTM_CHUNK

    @pl.when(pl.program_id(2) == 0)
    def _():
        ct_scr[...] = jnp.zeros_like(ct_scr)
        n_scr[...] = jnp.zeros_like(n_scr)
        m_scr[...] = jnp.zeros_like(m_scr)

    gc = gc_ref[...]
    lane = lax.broadcasted_iota(jnp.int32, gc.shape, 1)
    i_col = jnp.sum(jnp.where(lane == h, gc, 0.0), axis=1, keepdims=True)
    f_col = jnp.sum(jnp.where(lane == h + MLSTM_HEADS, gc, 0.0), axis=1, keepdims=True)
    i_row = gr_ref[pl.ds(h, 1), :]
    f_row = gr_ref[pl.ds(h + MLSTM_HEADS, 1), :]
    li_col, li_row = _softcap(i_col), _softcap(i_row)
    lf_col, lf_row = jax.nn.log_sigmoid(_softcap(f_col)), jax.nn.log_sigmoid(_softcap(f_row))

    t_idx = lax.broadcasted_iota(jnp.int32, (L, L), 0)
    s_idx = lax.broadcasted_iota(jnp.int32, (L, L), 1)
    causal = s_idx <= t_idx
    b_col = jnp.sum(jnp.where(causal, lf_row, 0.0), axis=1, keepdims=True)
    b_row = jnp.sum(jnp.where(causal, 0.0, lf_col) + jnp.where(s_idx == t_idx, lf_col, 0.0),
                    axis=0, keepdims=True)
    b_last = jnp.sum(lf_row, axis=1, keepdims=True)

    m_prev = m_scr[...]
    dmat = jnp.where(causal, b_col - b_row + li_row, NEG_BIG)
    inter = b_col + m_prev
    m_t = jnp.maximum(inter, jnp.max(dmat, axis=1, keepdims=True))
    q = q_ref[...]
    k = k_ref[...] * (MLSTM_DQK ** -0.5)
    v = v_ref[...]
    qb, kb, vb = q.astype(BF16), k.astype(BF16), v.astype(BF16)
    s = lax.dot_general(qb, kb, (((1,), (1,)), ((), ())), preferred_element_type=F32) * jnp.exp(dmat - m_t)
    w_inter = jnp.exp(inter - m_t)
    ct = ct_scr[...]
    n_row = n_scr[...]
    num = (jnp.dot(s.astype(BF16), vb, preferred_element_type=F32)
           + w_inter * jnp.dot(qb, ct.astype(BF16), preferred_element_type=F32))
    den = jnp.sum(s, axis=1, keepdims=True) + w_inter * jnp.sum(q * n_row, axis=1, keepdims=True)
    h_c = num / jnp.maximum(jnp.abs(den), jnp.exp(-m_t))

    d_col = b_last - b_col + li_col
    d_row = b_last - b_row + li_row
    m_new = jnp.maximum(b_last + m_prev, jnp.max(d_row, axis=1, keepdims=True))
    w_col = jnp.exp(d_col - m_new)
    w_c = jnp.exp(b_last + m_prev - m_new)
    kw = k * w_col
    ct_scr[...] = w_c * ct + jnp.dot(kw.T.astype(BF16), vb, preferred_element_type=F32)
    n_scr[...] = w_c * n_row + jnp.sum(kw, axis=0, keepdims=True)
    m_scr[...] = m_new

    hn = h_c * lax.rsqrt(jnp.mean(h_c * h_c, axis=-1, keepdims=True) + NORM_EPS) * hg_ref[...]
    out_ref[...] = (hn * jax.nn.sigmoid(o_ref[...])).astype(out_ref.dtype)


def mlstm_layer(x, norm_g, w_in, b_if, head_gain, w_out, batch, seq):
    m, d = x.shape
    H, L = MLSTM_HEADS, MLSTM_CHUNK
    nc = seq // L
    h = rmsnorm(x, norm_g)
    n_main = 2 * H * MLSTM_DQK + 2 * H * MLSTM_DV
    w_in_b = w_in.astype(BF16)
    proj = matmul(h, w_in_b[:, :n_main])
    gates = matmul(h, _pad_cols(w_in_b[:, n_main:], LANES), bias=_pad_cols(b_if.reshape(1, 2 * H), LANES))
    gates = gates[:, :2 * H]
    gates_t = jnp.transpose(gates.reshape(batch * nc, L, 2 * H), (0, 2, 1))
    qk_blocks = H * MLSTM_DQK // MLSTM_DQK
    out = pl.pallas_call(
        _mlstm_kernel,
        grid=(batch, H, nc),
        in_specs=[pl.BlockSpec((L, MLSTM_DQK), lambda b, hh, c: (b * nc + c, hh)),
                  pl.BlockSpec((L, MLSTM_DQK), lambda b, hh, c: (b * nc + c, qk_blocks + hh)),
                  pl.BlockSpec((L, MLSTM_DV), lambda b, hh, c: (b * nc + c, qk_blocks + hh)),
                  pl.BlockSpec((L, MLSTM_DV), lambda b, hh, c: (b * nc + c, qk_blocks + H + hh)),
                  pl.BlockSpec((L, 2 * H), lambda b, hh, c: (b * nc + c, 0)),
                  pl.BlockSpec((None, 2 * H, L), lambda b, hh, c: (b * nc + c, 0, 0)),
                  pl.BlockSpec((1, MLSTM_DV), lambda b, hh, c: (0, hh))],
        out_specs=pl.BlockSpec((L, MLSTM_DV), lambda b, hh, c: (b * nc + c, hh)),
        out_shape=jax.ShapeDtypeStruct((m, H * MLSTM_DV), BF16),
        scratch_shapes=[pltpu.VMEM((MLSTM_DQK, MLSTM_DV), F32), pltpu.VMEM((1, MLSTM_DQK), F32),
                        pltpu.VMEM((1, 1), F32)],
        compiler_params=_params("parallel", "parallel", "arbitrary"),
        name="mlstm_chunks",
    )(proj, proj, proj, proj, gates, gates_t, head_gain.reshape(1, -1))
    return matmul(out, w_out.astype(BF16), residual=x)


def kernel(x, moba_norm, moba_wqkv, moba_q_gain, moba_k_gain, moba_wo, rwkv_norm, rwkv_mu, rwkv_w_r, rwkv_w_k, rwkv_w_v, rwkv_w_o, rwkv_w0, rwkv_w1, rwkv_w2, rwkv_a0, rwkv_a1, rwkv_a2, rwkv_g1, rwkv_g2, rwkv_k_k, rwkv_k_a, rwkv_r_k, rwkv_lnx_w, rwkv_lnx_b, mlstm_norm, mlstm_w_in, mlstm_b_if, mlstm_head_gain, mlstm_w_out, ffn_norm, ffn_w_up, ffn_conv_w, ffn_conv_b, ffn_w_down):
    batch, seq, d = x.shape
    depth = ffn_norm.shape[0]
    x = x.reshape(batch * seq, d)
    for i in range(depth):
        kind, j = i % 3, i // 3
        if kind == 0:
            x = moba_layer(x, moba_norm[j], moba_wqkv[j], moba_q_gain[j], moba_k_gain[j], moba_wo[j], batch, seq)
        elif kind == 1:
            x = rwkv_layer(x, rwkv_norm[j], rwkv_mu[j], rwkv_w_r[j], rwkv_w_k[j], rwkv_w_v[j], rwkv_w_o[j],
                           rwkv_w0[j], rwkv_w1[j], rwkv_w2[j], rwkv_a0[j], rwkv_a1[j], rwkv_a2[j],
                           rwkv_g1[j], rwkv_g2[j], rwkv_k_k[j], rwkv_k_a[j], rwkv_r_k[j],
                           rwkv_lnx_w[j], rwkv_lnx_b[j], batch, seq)
        else:
            x = mlstm_layer(x, mlstm_norm[j], mlstm_w_in[j], mlstm_b_if[j], mlstm_head_gain[j],
                            mlstm_w_out[j], batch, seq)
        x = conv_ffn(x, ffn_norm[i], ffn_w_up[i], ffn_conv_w[i], ffn_conv_b[i], ffn_w_down[i], seq)
    return x.reshape(batch, seq, d)
```

```python
import functools

import jax
import jax.numpy as jnp
from jax import lax
from jax.experimental import pallas as pl
from jax.experimental.pallas import tpu as pltpu

F32 = jnp.float32
BF16 = jnp.bfloat16

LANES = 128
SUBLANES = 8
VMEM_LIMIT_BYTES = 56 * 1024 * 1024

NORM_EPS = 1e-6
MOBA_HEADS = 16
MOBA_HEAD_DIM = 128
MOBA_BLOCK = 256
MOBA_TOPK = 3
MOBA_HEADS_PER_CALL = 4
MOBA_Q_SCALE = 1.4426950408889634 * MOBA_HEAD_DIM ** -0.5
ROPE_THETA = 500000.0
ROT_DIM = MOBA_HEAD_DIM // 4
RWKV_HEAD_SIZE = 64
RWKV_GN_EPS = 64e-5
RWKV_CHUNK = 64
RWKV_GROUP = 4
MLSTM_HEADS = 8
MLSTM_DV = 256
MLSTM_DQK = 128
MLSTM_CHUNK = 64
GATE_SOFTCAP = 15.0
CONV_WIDTH = 3
NEG_BIG = -1e30

HI = lax.Precision.HIGHEST


def _params(*semantics):
    return pltpu.CompilerParams(dimension_semantics=semantics, vmem_limit_bytes=VMEM_LIMIT_BYTES)


def _pad_cols(w, n):
    return jnp.pad(w, ((0, 0), (0, n - w.shape[1])))


def _pad_rows(w, n):
    return jnp.pad(w, ((0, n - w.shape[0]), (0, 0)))


def _rmsnorm_kernel(x_ref, g_ref, o_ref):
    x = x_ref[...]
    y = x * lax.rsqrt(jnp.mean(x * x, axis=-1, keepdims=True) + NORM_EPS)
    o_ref[...] = (y * g_ref[...]).astype(o_ref.dtype)


def rmsnorm(x, gain, tm=512):
    m, d = x.shape
    return pl.pallas_call(
        _rmsnorm_kernel,
        grid=(m // tm,),
        in_specs=[pl.BlockSpec((tm, d), lambda i: (i, 0)), pl.BlockSpec((1, d), lambda i: (0, 0))],
        out_specs=pl.BlockSpec((tm, d), lambda i: (i, 0)),
        out_shape=jax.ShapeDtypeStruct((m, d), BF16),
        compiler_params=_params("parallel"),
        name="rmsnorm",
    )(x, gain.reshape(1, d))


def _mm_kernel(*refs, act, has_bias, has_res):
    x_ref, w_ref = refs[0], refs[1]
    o_ref = refs[-1]
    acc = jnp.dot(x_ref[...], w_ref[...], preferred_element_type=F32)
    k = 2
    if has_bias:
        acc = acc + refs[k][...]
        k += 1
    if act == "tanh":
        acc = jnp.tanh(acc)
    elif act == "sigmoid":
        acc = jax.nn.sigmoid(acc)
    if has_res:
        acc = acc + refs[k][...]
    o_ref[...] = acc.astype(o_ref.dtype)


def matmul(x, w, *, bias=None, act=None, residual=None, out_dtype=F32, tm=1024, tn=1024):
    m, kdim = x.shape
    n = w.shape[1]
    tm, tn = min(tm, m), min(tn, n)
    assert m % tm == 0 and n % tn == 0, (m, n, tm, tn)
    in_specs = [pl.BlockSpec((tm, kdim), lambda j, i: (i, 0)), pl.BlockSpec((kdim, tn), lambda j, i: (0, j))]
    args = [x, w]
    if bias is not None:
        in_specs.append(pl.BlockSpec((1, tn), lambda j, i: (0, j)))
        args.append(bias.reshape(1, n))
    if residual is not None:
        in_specs.append(pl.BlockSpec((tm, tn), lambda j, i: (i, j)))
        args.append(residual)
    return pl.pallas_call(
        functools.partial(_mm_kernel, act=act, has_bias=bias is not None, has_res=residual is not None),
        grid=(n // tn, m // tm),
        in_specs=in_specs,
        out_specs=pl.BlockSpec((tm, tn), lambda j, i: (i, j)),
        out_shape=jax.ShapeDtypeStruct((m, n), out_dtype),
        compiler_params=_params("parallel", "parallel"),
        name="matmul",
    )(*args)


def _ffn_up_kernel(x_ref, h_ref, wg_ref, wu_ref, cwg_ref, cwu_ref, cbg_ref, cbu_ref, o_ref, *, tiles_per_seq):
    i = pl.program_id(1)
    first = (i % tiles_per_seq) == 0
    x = x_ref[...]
    halo = h_ref[...]
    tm = x.shape[0]
    tn = o_ref.shape[1]
    row = lax.broadcasted_iota(jnp.int32, (tm, tn), 0)

    def causal_conv(w_ref, cw_ref, cb_ref):
        u = jnp.dot(x, w_ref[...], preferred_element_type=F32)
        uh = jnp.dot(halo, w_ref[...], preferred_element_type=F32)
        uh = jnp.where(first, 0.0, uh)
        u1 = jnp.where(row == 0, uh[7:8, :], pltpu.roll(u, 1, 0))
        u2 = pltpu.roll(u, 2, 0)
        u2 = jnp.where(row == 0, uh[6:7, :], jnp.where(row == 1, uh[7:8, :], u2))
        cw = cw_ref[...]
        return u2 * cw[0:1, :] + u1 * cw[1:2, :] + u * cw[2:3, :] + cb_ref[...]

    gate = causal_conv(wg_ref, cwg_ref, cbg_ref)
    up = causal_conv(wu_ref, cwu_ref, cbu_ref)
    o_ref[...] = (gate * jax.nn.sigmoid(gate) * up).astype(o_ref.dtype)


def ffn_up(h, w_up, conv_w, conv_b, seq, tm=1024, tn=512):
    m, d = h.shape
    f = w_up.shape[1] // 2
    nf = f // tn
    assert f % tn == 0 and seq % tm == 0
    hb = tm // SUBLANES
    return pl.pallas_call(
        functools.partial(_ffn_up_kernel, tiles_per_seq=seq // tm),
        grid=(nf, m // tm),
        in_specs=[
            pl.BlockSpec((tm, d), lambda c, i: (i, 0)),
            pl.BlockSpec((SUBLANES, d), lambda c, i: (jnp.maximum(i * hb - 1, 0), 0)),
            pl.BlockSpec((d, tn), lambda c, i: (0, c)),
            pl.BlockSpec((d, tn), lambda c, i: (0, c + nf)),
            pl.BlockSpec((CONV_WIDTH, tn), lambda c, i: (0, c)),
            pl.BlockSpec((CONV_WIDTH, tn), lambda c, i: (0, c + nf)),
            pl.BlockSpec((1, tn), lambda c, i: (0, c)),
            pl.BlockSpec((1, tn), lambda c, i: (0, c + nf)),
        ],
        out_specs=pl.BlockSpec((tm, tn), lambda c, i: (i, c)),
        out_shape=jax.ShapeDtypeStruct((m, f), BF16),
        compiler_params=_params("parallel", "parallel"),
        name="ffn_up_conv",
    )(h, h, w_up, w_up, conv_w, conv_w, conv_b.reshape(1, 2 * f), conv_b.reshape(1, 2 * f))


def conv_ffn(x, norm_g, w_up, conv_w, conv_b, w_down, seq):
    h = rmsnorm(x, norm_g)
    act = ffn_up(h, w_up.astype(BF16), conv_w, conv_b, seq)
    return matmul(act, w_down.astype(BF16), residual=x, tm=512, tn=512)


def _rope_tables(seq):
    half = ROT_DIM // 2
    inv_freq = jnp.float32(ROPE_THETA) ** (-jnp.arange(0, ROT_DIM, 2, dtype=F32) / ROT_DIM)
    ang = jnp.arange(seq, dtype=F32)[:, None] * inv_freq[None, :]
    cos, sin = jnp.cos(ang), jnp.sin(ang)
    ones = jnp.ones((seq, MOBA_HEAD_DIM - ROT_DIM), F32)
    zeros_h = jnp.zeros((seq, half), F32)
    zeros_r = jnp.zeros((seq, MOBA_HEAD_DIM - ROT_DIM), F32)
    c_tab = jnp.concatenate([cos, cos, ones], axis=1)
    s_lo = jnp.concatenate([-sin, zeros_h, zeros_r], axis=1)
    s_hi = jnp.concatenate([zeros_h, sin, zeros_r], axis=1)
    return c_tab, s_lo, s_hi


def _moba_prep_kernel(qkv_ref, c_ref, slo_ref, shi_ref, qg_ref, kg_ref,
                      q_ref, k_ref, vt_ref, sel_ref, kmean_scr, *, blocks_per_seq):
    n = pl.program_id(0) % blocks_per_seq
    d = MOBA_HEADS * MOBA_HEAD_DIM
    half = ROT_DIM // 2
    c_tab, s_lo, s_hi = c_ref[...], slo_ref[...], shi_ref[...]

    @pl.when(n == 0)
    def _():
        kmean_scr[...] = jnp.zeros_like(kmean_scr)

    def norm_rope(x, gain):
        y = x * lax.rsqrt(jnp.mean(x * x, axis=-1, keepdims=True) + NORM_EPS) * gain
        return (y * c_tab + pltpu.roll(y, MOBA_HEAD_DIM - half, 1) * s_lo + pltpu.roll(y, half, 1) * s_hi)

    nb = sel_ref.shape[1]
    blk = lax.broadcasted_iota(jnp.int32, (nb, MOBA_BLOCK), 0)
    eligible = blk < n
    for h in range(MOBA_HEADS):
        lo = h * MOBA_HEAD_DIM
        q = norm_rope(qkv_ref[:, lo:lo + MOBA_HEAD_DIM], qg_ref[...])
        k = norm_rope(qkv_ref[:, d + lo:d + lo + MOBA_HEAD_DIM], kg_ref[...])
        q_ref[:, lo:lo + MOBA_HEAD_DIM] = (q * MOBA_Q_SCALE).astype(q_ref.dtype)
        k_ref[:, lo:lo + MOBA_HEAD_DIM] = k.astype(k_ref.dtype)
        vt_ref[lo:lo + MOBA_HEAD_DIM, :] = qkv_ref[:, 2 * d + lo:2 * d + lo + MOBA_HEAD_DIM].T.astype(vt_ref.dtype)
        gate = lax.dot_general(kmean_scr[:, lo:lo + MOBA_HEAD_DIM], q, (((1,), (1,)), ((), ())),
                               precision=HI, preferred_element_type=F32)
        gate = jnp.where(eligible, gate, -jnp.inf)
        sel = jnp.zeros((nb, MOBA_BLOCK), F32)
        for j in range(nb):
            gj = gate[j:j + 1, :]
            beats = jnp.where(gate > gj, 1.0, jnp.where((gate == gj) & (blk < j), 1.0, 0.0))
            rank = jnp.sum(beats, axis=0, keepdims=True)
            chosen = jnp.where((rank < MOBA_TOPK) & (j < n), 1.0, 0.0)
            sel = jnp.where(blk == j, chosen, sel)
        sel_ref[h] = sel
        km_rows = lax.broadcasted_iota(jnp.int32, (nb, MOBA_HEAD_DIM), 0)
        kmean_scr[:, lo:lo + MOBA_HEAD_DIM] = jnp.where(km_rows == n, jnp.mean(k, axis=0, keepdims=True),
                                                        kmean_scr[:, lo:lo + MOBA_HEAD_DIM])


def _moba_attn_kernel(q_ref, k_ref, vt_ref, sel_ref, o_ref):
    i = pl.program_id(2)
    heads = range(MOBA_HEADS_PER_CALL)
    lanes = [slice(h * MOBA_HEAD_DIM, (h + 1) * MOBA_HEAD_DIM) for h in heads]
    q = [q_ref[:, ln] for ln in lanes]
    nt_dims = (((1,), (1,)), ((), ()))

    def scores_t(start, h):
        return lax.dot_general(k_ref[pl.ds(start, MOBA_BLOCK), lanes[h]], q[h], nt_dims, preferred_element_type=F32)

    def weighted_values(start, h, p):
        return jnp.dot(vt_ref[lanes[h], pl.ds(start, MOBA_BLOCK)], p.astype(BF16), preferred_element_type=F32)

    own = pl.multiple_of(i * MOBA_BLOCK, MOBA_BLOCK)
    kpos = lax.broadcasted_iota(jnp.int32, (MOBA_BLOCK, MOBA_BLOCK), 0)
    qpos = lax.broadcasted_iota(jnp.int32, (MOBA_BLOCK, MOBA_BLOCK), 1)
    causal = kpos <= qpos
    s_own = [jnp.where(causal, scores_t(own, h), NEG_BIG) for h in heads]
    m0 = [jnp.max(x, axis=0, keepdims=True) for x in s_own]
    p0 = [jnp.exp2(s_own[h] - m0[h]) for h in heads]
    l0 = [jnp.sum(x, axis=0, keepdims=True) for x in p0]
    acc0 = [weighted_values(own, h, p0[h]) for h in heads]

    def body(j, carry):
        m_run, l_run, acc = carry
        start = pl.multiple_of(j * MOBA_BLOCK, MOBA_BLOCK)
        s_j = [jnp.where(sel_ref[h, pl.ds(j, 1), :] > 0.5, scores_t(start, h), NEG_BIG) for h in heads]
        m_new = [jnp.maximum(m_run[h], jnp.max(s_j[h], axis=0, keepdims=True)) for h in heads]
        alpha = [jnp.exp2(m_run[h] - m_new[h]) for h in heads]
        p_j = [jnp.exp2(s_j[h] - m_new[h]) for h in heads]
        l_new = [alpha[h] * l_run[h] + jnp.sum(p_j[h], axis=0, keepdims=True) for h in heads]
        acc = [alpha[h] * acc[h] + weighted_values(start, h, p_j[h]) for h in heads]
        return tuple(m_new), tuple(l_new), tuple(acc)

    _, l_fin, acc = lax.fori_loop(0, i, body, (tuple(m0), tuple(l0), tuple(acc0)))
    for h in heads:
        o_ref[:, lanes[h]] = (acc[h] / l_fin[h]).T.astype(o_ref.dtype)


def moba_layer(x, norm_g, wqkv, q_gain, k_gain, wo, batch, seq):
    m, d = x.shape
    nb = seq // MOBA_BLOCK
    h = rmsnorm(x, norm_g)
    qkv = matmul(h, wqkv.astype(BF16))
    c_tab, s_lo, s_hi = _rope_tables(seq)
    tab_spec = pl.BlockSpec((MOBA_BLOCK, MOBA_HEAD_DIM), lambda i: (i % nb, 0))
    gain_spec = pl.BlockSpec((1, MOBA_HEAD_DIM), lambda i: (0, 0))
    q, k, vt, sel = pl.pallas_call(
        functools.partial(_moba_prep_kernel, blocks_per_seq=nb),
        grid=(m // MOBA_BLOCK,),
        in_specs=[pl.BlockSpec((MOBA_BLOCK, 3 * d), lambda i: (i, 0)), tab_spec, tab_spec, tab_spec,
                  gain_spec, gain_spec],
        out_specs=[pl.BlockSpec((MOBA_BLOCK, d), lambda i: (i, 0)),
                   pl.BlockSpec((MOBA_BLOCK, d), lambda i: (i, 0)),
                   pl.BlockSpec((d, MOBA_BLOCK), lambda i: (0, i)),
                   pl.BlockSpec((None, MOBA_HEADS, nb, MOBA_BLOCK), lambda i: (i, 0, 0, 0))],
        out_shape=[jax.ShapeDtypeStruct((m, d), BF16), jax.ShapeDtypeStruct((m, d), BF16),
                   jax.ShapeDtypeStruct((d, m), BF16),
                   jax.ShapeDtypeStruct((m // MOBA_BLOCK, MOBA_HEADS, nb, MOBA_BLOCK), F32)],
        scratch_shapes=[pltpu.VMEM((nb, d), F32)],
        compiler_params=_params("arbitrary"),
        name="moba_prep",
    )(qkv, c_tab, s_lo, s_hi, q_gain.reshape(1, -1), k_gain.reshape(1, -1))
    hw = MOBA_HEADS_PER_CALL * MOBA_HEAD_DIM
    attn = pl.pallas_call(
        _moba_attn_kernel,
        grid=(batch, MOBA_HEADS // MOBA_HEADS_PER_CALL, nb),
        in_specs=[pl.BlockSpec((MOBA_BLOCK, hw), lambda b, hh, i: (b * nb + i, hh)),
                  pl.BlockSpec((seq, hw), lambda b, hh, i: (b, hh)),
                  pl.BlockSpec((hw, seq), lambda b, hh, i: (hh, b)),
                  pl.BlockSpec((None, MOBA_HEADS_PER_CALL, nb, MOBA_BLOCK), lambda b, hh, i: (b * nb + i, hh, 0, 0))],
        out_specs=pl.BlockSpec((MOBA_BLOCK, hw), lambda b, hh, i: (b * nb + i, hh)),
        out_shape=jax.ShapeDtypeStruct((m, d), BF16),
        compiler_params=_params("parallel", "parallel", "parallel"),
        name="moba_attn",
    )(q, k, vt, sel)
    return matmul(attn, wo.astype(BF16), residual=x)


def _head_sum_matrix():
    r = lax.broadcasted_iota(jnp.int32, (LANES, LANES), 0) // RWKV_HEAD_SIZE
    c = lax.broadcasted_iota(jnp.int32, (LANES, LANES), 1) // RWKV_HEAD_SIZE
    return jnp.where(r == c, 1.0, 0.0).astype(F32)


def _head_sum(x, ones_bd):
    return jnp.dot(x, ones_bd, precision=HI, preferred_element_type=F32)


def _rwkv_mix_kernel(x_ref, h_ref, g_ref, mu_ref, *o_refs, tiles_per_seq):
    first = (pl.program_id(0) % tiles_per_seq) == 0

    def norm(x):
        return (x * lax.rsqrt(jnp.mean(x * x, axis=-1, keepdims=True) + NORM_EPS) * g_ref[...])

    h = norm(x_ref[...])
    prev_row = jnp.where(first, 0.0, norm(h_ref[...])[SUBLANES - 1:SUBLANES, :])
    row = lax.broadcasted_iota(jnp.int32, h.shape, 0)
    xx = jnp.where(row == 0, prev_row, pltpu.roll(h, 1, 0)) - h
    for idx, o_ref in enumerate(o_refs):
        o_ref[...] = (h + xx * mu_ref[idx:idx + 1, :]).astype(o_ref.dtype)


def _rwkv_prep_kernel(k_ref, wp_ref, ap_ref, w0_ref, a0_ref, kk_ref, ka_ref,
                      logw_ref, kmod_ref, an_ref, b_ref):
    ones_bd = _head_sum_matrix()
    w_log = -jax.nn.softplus(-(w0_ref[...] + wp_ref[...])) - 0.5
    logw_ref[...] = -jnp.exp(w_log)
    a = jax.nn.sigmoid(a0_ref[...] + ap_ref[...])
    k = k_ref[...]
    kmod_ref[...] = k * (1.0 + (a - 1.0) * ka_ref[...])
    kk = k * kk_ref[...]
    for c in range(kk.shape[1] // LANES):
        sl = slice(c * LANES, (c + 1) * LANES)
        kc = kk[:, sl]
        nrm = jnp.maximum(jnp.sqrt(_head_sum(kc * kc, ones_bd)), 1e-12)
        kc = kc / nrm
        an_ref[:, sl] = -kc
        b_ref[:, sl] = kc * a[:, sl]


def _rwkv_chunk_kernel(r_ref, lw_ref, k_ref, v_ref, a_ref, b_ref, y_ref, s_scr):
    L, n, grp = RWKV_CHUNK, RWKV_HEAD_SIZE, RWKV_GROUP
    width = grp * n
    rows = grp * L
    tb = r_ref.shape[0]

    @pl.when(pl.program_id(2) == 0)
    def _():
        s_scr[...] = jnp.zeros_like(s_scr)

    row = lax.broadcasted_iota(jnp.int32, (rows, rows), 0)
    col = lax.broadcasted_iota(jnp.int32, (rows, rows), 1)
    strict = (col % L) < (row % L)
    incl = (col % L) <= (row % L)
    lane_head = lax.broadcasted_iota(jnp.int32, (L, width), 1) // n
    cum = jnp.where(lax.broadcasted_iota(jnp.int32, (L, L), 1) <= lax.broadcasted_iota(jnp.int32, (L, L), 0),
                    1.0, 0.0).astype(F32)
    nt_dims = (((1,), (1,)), ((), ()))
    tn_dims = (((0,), (0,)), ((), ()))

    def stack(x):
        return jnp.concatenate([jnp.where(lane_head == h, x, 0.0) for h in range(grp)], axis=0)

    def unstack(xm):
        out = xm[0:L]
        for h in range(1, grp):
            out = out + xm[h * L:(h + 1) * L]
        return out

    def mm(x, y):
        return jnp.dot(x.astype(BF16), y.astype(BF16), preferred_element_type=F32)

    n_groups = r_ref.shape[1] // width
    groups = range(n_groups)

    def chunk(c, carry):
        sl = pl.ds(pl.multiple_of(c * L, L), L)
        lanes = [slice(q * width, (q + 1) * width) for q in groups]
        lw = [lw_ref[sl, ln] for ln in lanes]
        g = [jnp.dot(cum, x, precision=HI, preferred_element_type=F32) for x in lw]
        g_last = [x[L - 1:L, :] for x in g]
        e_neg = [jnp.exp(-x) for x in g]
        r, k, v, a, b = ([ref[sl, ln] for ln in lanes] for ref in (r_ref, k_ref, v_ref, a_ref, b_ref))
        lhs = [jnp.concatenate([stack(a[q] * jnp.exp(g[q] - lw[q])), stack(r[q] * jnp.exp(g[q]))], axis=0).astype(BF16)
               for q in groups]
        rhs = [jnp.concatenate([stack(b[q] * e_neg[q]), stack(k[q] * e_neg[q])], axis=0).astype(BF16) for q in groups]
        v_m = [stack(x) for x in v]
        s = [s_scr[q] for q in groups]
        prod = [lax.dot_general(lhs[q], rhs[q], nt_dims, preferred_element_type=F32) for q in groups]
        from_state = [lax.dot_general(lhs[q], s[q].astype(BF16), nt_dims, preferred_element_type=F32) for q in groups]
        a_ak = [jnp.where(strict, p[:rows, rows:], 0.0) for p in prod]
        u_m = [from_state[q][:rows] + mm(a_ak[q], v_m[q]) for q in groups]
        power = [jnp.where(strict, p[:rows, :rows], 0.0) for p in prod]
        for level in range(L.bit_length() - 1):
            if level:
                power = [mm(p, p) for p in power]
            u_m = [u_m[q] + mm(power[q], u_m[q]) for q in groups]
        for q in groups:
            m_both = jnp.where(jnp.concatenate([incl, incl], axis=1), prod[q][rows:, :], 0.0).astype(BF16)
            uv = jnp.concatenate([u_m[q], v_m[q]], axis=0).astype(BF16)
            y_ref[sl, lanes[q]] = unstack(from_state[q][rows:] + jnp.dot(m_both, uv, preferred_element_type=F32))
            e_tail = jnp.exp(g_last[q] - g[q])
            tail = jnp.concatenate([stack(b[q] * e_tail), stack(k[q] * e_tail)], axis=0).astype(BF16)
            s_scr[q] = s[q] * jnp.exp(g_last[q]) + lax.dot_general(uv, tail, tn_dims, preferred_element_type=F32)
        return carry

    lax.fori_loop(0, tb // L, chunk, 0)


def _rwkv_post_kernel(y_ref, r_ref, kmod_ref, v_ref, g_ref, rk_ref, lw_ref, lb_ref, o_ref):
    ones_bd = _head_sum_matrix()
    inv_n = 1.0 / RWKV_HEAD_SIZE
    for c in range(y_ref.shape[1] // LANES):
        sl = slice(c * LANES, (c + 1) * LANES)
        y = y_ref[:, sl]
        mean = _head_sum(y, ones_bd) * inv_n
        yc = y - mean
        var = _head_sum(yc * yc, ones_bd) * inv_n
        yn = yc * lax.rsqrt(var + RWKV_GN_EPS) * lw_ref[:, sl] + lb_ref[:, sl]
        bonus = _head_sum(r_ref[:, sl] * kmod_ref[:, sl] * rk_ref[:, sl], ones_bd) * v_ref[:, sl]
        o_ref[:, sl] = ((yn + bonus) * g_ref[:, sl]).astype(o_ref.dtype)


def rwkv_layer(x, norm_g, mu, w_r, w_k, w_v, w_o, w0, w1, w2, a0, a1, a2, g1, g2,
               k_k, k_a, r_k, lnx_w, lnx_b, batch, seq):
    m, d = x.shape
    tm = 256
    row_spec = pl.BlockSpec((tm, d), lambda i: (i, 0))
    vec_spec = pl.BlockSpec((1, d), lambda i: (0, 0))
    hb = tm // SUBLANES
    mixed = pl.pallas_call(
        functools.partial(_rwkv_mix_kernel, tiles_per_seq=seq // tm),
        grid=(m // tm,),
        in_specs=[row_spec, pl.BlockSpec((SUBLANES, d), lambda i: (jnp.maximum(i * hb - 1, 0), 0)),
                  vec_spec, pl.BlockSpec((6, d), lambda i: (0, 0))],
        out_specs=[row_spec] * 6,
        out_shape=[jax.ShapeDtypeStruct((m, d), BF16)] * 6,
        compiler_params=_params("parallel"),
        name="rwkv_mix",
    )(x, x, norm_g.reshape(1, d), mu)
    x_r, x_w, x_k, x_v, x_a, x_g = mixed

    def lora_pad(w_in, w_out):
        rank = -(-w_in.shape[1] // LANES) * LANES
        return _pad_cols(w_in, rank).astype(BF16), _pad_rows(w_out, rank).astype(BF16)

    r = matmul(x_r, w_r.astype(BF16))
    k = matmul(x_k, w_k.astype(BF16))
    v = matmul(x_v, w_v.astype(BF16))
    w1p, w2p = lora_pad(w1, w2)
    a1p, a2p = lora_pad(a1, a2)
    g1p, g2p = lora_pad(g1, g2)
    w_pre = matmul(matmul(x_w, w1p, act="tanh", out_dtype=BF16), w2p)
    a_pre = matmul(matmul(x_a, a1p, out_dtype=BF16), a2p)
    gate = matmul(matmul(x_g, g1p, act="sigmoid", out_dtype=BF16), g2p)

    log_w, k_mod, a_neg, b_vec = pl.pallas_call(
        _rwkv_prep_kernel,
        grid=(m // tm,),
        in_specs=[row_spec] * 3 + [vec_spec] * 4,
        out_specs=[row_spec] * 4,
        out_shape=[jax.ShapeDtypeStruct((m, d), F32)] * 4,
        compiler_params=_params("parallel"),
        name="rwkv_prep",
    )(k, w_pre, a_pre, w0.reshape(1, d), a0.reshape(1, d), k_k.reshape(1, d), k_a.reshape(1, d))

    tb, groups_per_call = 256, 4
    width = groups_per_call * RWKV_GROUP * RWKV_HEAD_SIZE
    nt = seq // tb
    blk = pl.BlockSpec((tb, width), lambda b, p, t: (b * nt + t, p))
    y = pl.pallas_call(
        _rwkv_chunk_kernel,
        grid=(batch, d // width, nt),
        in_specs=[blk] * 6,
        out_specs=blk,
        out_shape=jax.ShapeDtypeStruct((m, d), F32),
        scratch_shapes=[pltpu.VMEM((groups_per_call, RWKV_GROUP * RWKV_HEAD_SIZE, RWKV_GROUP * RWKV_HEAD_SIZE), F32)],
        compiler_params=_params("parallel", "parallel", "arbitrary"),
        name="rwkv_chunks",
    )(r, log_w, k_mod, v, a_neg, b_vec)

    out = pl.pallas_call(
        _rwkv_post_kernel,
        grid=(m // tm,),
        in_specs=[row_spec] * 5 + [vec_spec] * 3,
        out_specs=row_spec,
        out_shape=jax.ShapeDtypeStruct((m, d), BF16),
        compiler_params=_params("parallel"),
        name="rwkv_post",
    )(y, r, k_mod, v, gate, r_k.reshape(1, d), lnx_w.reshape(1, d), lnx_b.reshape(1, d))
    return matmul(out, w_o.astype(BF16), residual=x)


def _softcap(z):
    return GATE_SOFTCAP * jnp.tanh(z / GATE_SOFTCAP)


def _mlstm_kernel(q_ref, k_ref, v_ref, o_ref, gc_ref, gr_ref, hg_ref, out_ref, ct_scr, n_scr, m_scr):
    h = pl.program_id(1)
    L = MLSTM_CHUNK

    @pl.when(pl.program_id(2) == 0)
    def _():
        ct_scr[...] = jnp.zeros_like(ct_scr)
        n_scr[...] = jnp.zeros_like(n_scr)
        m_scr[...] = jnp.zeros_like(m_scr)

    gc = gc_ref[...]
    lane = lax.broadcasted_iota(jnp.int32, gc.shape, 1)
    i_col = jnp.sum(jnp.where(lane == h, gc, 0.0), axis=1, keepdims=True)
    f_col = jnp.sum(jnp.where(lane == h + MLSTM_HEADS, gc, 0.0), axis=1, keepdims=True)
    i_row = gr_ref[pl.ds(h, 1), :]
    f_row = gr_ref[pl.ds(h + MLSTM_HEADS, 1), :]
    li_col, li_row = _softcap(i_col), _softcap(i_row)
    lf_col, lf_row = jax.nn.log_sigmoid(_softcap(f_col)), jax.nn.log_sigmoid(_softcap(f_row))

    t_idx = lax.broadcasted_iota(jnp.int32, (L, L), 0)
    s_idx = lax.broadcasted_iota(jnp.int32, (L, L), 1)
    causal = s_idx <= t_idx
    b_col = jnp.sum(jnp.where(causal, lf_row, 0.0), axis=1, keepdims=True)
    b_row = jnp.sum(jnp.where(causal, 0.0, lf_col) + jnp.where(s_idx == t_idx, lf_col, 0.0),
                    axis=0, keepdims=True)
    b_last = jnp.sum(lf_row, axis=1, keepdims=True)

    m_prev = m_scr[...]
    dmat = jnp.where(causal, b_col - b_row + li_row, NEG_BIG)
    inter = b_col + m_prev
    m_t = jnp.maximum(inter, jnp.max(dmat, axis=1, keepdims=True))
    q = q_ref[...]
    k = k_ref[...] * (MLSTM_DQK ** -0.5)
    v = v_ref[...]
    qb, kb, vb = q.astype(BF16), k.astype(BF16), v.astype(BF16)
    s = lax.dot_general(qb, kb, (((1,), (1,)), ((), ())), preferred_element_type=F32) * jnp.exp(dmat - m_t)
    w_inter = jnp.exp(inter - m_t)
    ct = ct_scr[...]
    n_row = n_scr[...]
    num = (jnp.dot(s.astype(BF16), vb, preferred_element_type=F32)
           + w_inter * jnp.dot(qb, ct.astype(BF16), preferred_element_type=F32))
    den = jnp.sum(s, axis=1, keepdims=True) + w_inter * jnp.sum(q * n_row, axis=1, keepdims=True)
    h_c = num / jnp.maximum(jnp.abs(den), jnp.exp(-m_t))

    d_col = b_last - b_col + li_col
    d_row = b_last - b_row + li_row
    m_new = jnp.maximum(b_last + m_prev, jnp.max(d_row, axis=1, keepdims=True))
    w_col = jnp.exp(d_col - m_new)
    w_c = jnp.exp(b_last + m_prev - m_new)
    kw = k * w_col
    ct_scr[...] = w_c * ct + jnp.dot(kw.T.astype(BF16), vb, preferred_element_type=F32)
    n_scr[...] = w_c * n_row + jnp.sum(kw, axis=0, keepdims=True)
    m_scr[...] = m_new

    hn = h_c * lax.rsqrt(jnp.mean(h_c * h_c, axis=-1, keepdims=True) + NORM_EPS) * hg_ref[...]
    out_ref[...] = (hn * jax.nn.sigmoid(o_ref[...])).astype(out_ref.dtype)


def mlstm_layer(x, norm_g, w_in, b_if, head_gain, w_out, batch, seq):
    m, d = x.shape
    H, L = MLSTM_HEADS, MLSTM_CHUNK
    nc = seq // L
    h = rmsnorm(x, norm_g)
    n_main = 2 * H * MLSTM_DQK + 2 * H * MLSTM_DV
    w_in_b = w_in.astype(BF16)
    proj = matmul(h, w_in_b[:, :n_main])
    gates = matmul(h, _pad_cols(w_in_b[:, n_main:], LANES), bias=_pad_cols(b_if.reshape(1, 2 * H), LANES))
    gates = gates[:, :2 * H]
    gates_t = jnp.transpose(gates.reshape(batch * nc, L, 2 * H), (0, 2, 1))
    qk_blocks = H * MLSTM_DQK // MLSTM_DQK
    out = pl.pallas_call(
        _mlstm_kernel,
        grid=(batch, H, nc),
        in_specs=[pl.BlockSpec((L, MLSTM_DQK), lambda b, hh, c: (b * nc + c, hh)),
                  pl.BlockSpec((L, MLSTM_DQK), lambda b, hh, c: (b * nc + c, qk_blocks + hh)),
                  pl.BlockSpec((L, MLSTM_DV), lambda b, hh, c: (b * nc + c, qk_blocks + hh)),
                  pl.BlockSpec((L, MLSTM_DV), lambda b, hh, c: (b * nc + c, qk_blocks + H + hh)),
                  pl.BlockSpec((L, 2 * H), lambda b, hh, c: (b * nc + c, 0)),
                  pl.BlockSpec((None, 2 * H, L), lambda b, hh, c: (b * nc + c, 0, 0)),
                  pl.BlockSpec((1, MLSTM_DV), lambda b, hh, c: (0, hh))],
        out_specs=pl.BlockSpec((L, MLSTM_DV), lambda b, hh, c: (b * nc + c, hh)),
        out_shape=jax.ShapeDtypeStruct((m, H * MLSTM_DV), BF16),
        scratch_shapes=[pltpu.VMEM((MLSTM_DQK, MLSTM_DV), F32), pltpu.VMEM((1, MLSTM_DQK), F32),
                        pltpu.VMEM((1, 1), F32)],
        compiler_params=_params("parallel", "parallel", "arbitrary"),
        name="mlstm_chunks",
    )(proj, proj, proj, proj, gates, gates_t, head_gain.reshape(1, -1))
    return matmul(out, w_out.astype(BF16), residual=x)


def kernel(x, moba_norm, moba_wqkv, moba_q_gain, moba_k_gain, moba_wo, rwkv_norm, rwkv_mu, rwkv_w_r, rwkv_w_k, rwkv_w_v, rwkv_w_o, rwkv_w0, rwkv_w1, rwkv_w2, rwkv_a0, rwkv_a1, rwkv_a2, rwkv_g1, rwkv_g2, rwkv_k_k, rwkv_k_a, rwkv_r_k, rwkv_lnx_w, rwkv_lnx_b, mlstm_norm, mlstm_w_in, mlstm_b_if, mlstm_head_gain, mlstm_w_out, ffn_norm, ffn_w_up, ffn_conv_w, ffn_conv_b, ffn_w_down):
    batch, seq, d = x.shape
    depth = ffn_norm.shape[0]
    x = x.reshape(batch * seq, d)
    for i in range(depth):
        kind, j = i % 3, i // 3
        if kind == 0:
            x = moba_layer(x, moba_norm[j], moba_wqkv[j], moba_q_gain[j], moba_k_gain[j], moba_wo[j], batch, seq)
        elif kind == 1:
            x = rwkv_layer(x, rwkv_norm[j], rwkv_mu[j], rwkv_w_r[j], rwkv_w_k[j], rwkv_w_v[j], rwkv_w_o[j],
                           rwkv_w0[j], rwkv_w1[j], rwkv_w2[j], rwkv_a0[j], rwkv_a1[j], rwkv_a2[j],
                           rwkv_g1[j], rwkv_g2[j], rwkv_k_k[j], rwkv_k_a[j], rwkv_r_k[j],
                           rwkv_lnx_w[j], rwkv_lnx_b[j], batch, seq)
        else:
            x = mlstm_layer(x, mlstm_norm[j], mlstm_w_in[j], mlstm_b_if[j], mlstm_head_gain[j],
                            mlstm_w_out[j], batch, seq)
        x = conv_ffn(x, ffn_norm[i], ffn_w_up[i], ffn_conv_w[i], ffn_conv_b[i], ffn_w_down[i], seq)
    return x.reshape(batch, seq, d)
```

```python
import functools

import jax
import jax.numpy as jnp
from jax import lax
from jax.experimental import pallas as pl
from jax.experimental.pallas import tpu as pltpu

F32 = jnp.float32
BF16 = jnp.bfloat16

LANES = 128
SUBLANES = 8
VMEM_LIMIT_BYTES = 56 * 1024 * 1024

NORM_EPS = 1e-6
MOBA_HEADS = 16
MOBA_HEAD_DIM = 128
MOBA_BLOCK = 256
MOBA_TOPK = 3
MOBA_HEADS_PER_CALL = 4
MOBA_Q_SCALE = 1.4426950408889634 * MOBA_HEAD_DIM ** -0.5
ROPE_THETA = 500000.0
ROT_DIM = MOBA_HEAD_DIM // 4
RWKV_HEAD_SIZE = 64
RWKV_GN_EPS = 64e-5
RWKV_CHUNK = 64
RWKV_GROUP = 4
MLSTM_HEADS = 8
MLSTM_DV = 256
MLSTM_DQK = 128
MLSTM_CHUNK = 64
GATE_SOFTCAP = 15.0
CONV_WIDTH = 3
NEG_BIG = -1e30

HI = lax.Precision.HIGHEST


def _params(*semantics):
    return pltpu.CompilerParams(dimension_semantics=semantics, vmem_limit_bytes=VMEM_LIMIT_BYTES)


def _pad_cols(w, n):
    return jnp.pad(w, ((0, 0), (0, n - w.shape[1])))


def _pad_rows(w, n):
    return jnp.pad(w, ((0, n - w.shape[0]), (0, 0)))


def _rmsnorm_kernel(x_ref, g_ref, o_ref):
    x = x_ref[...]
    y = x * lax.rsqrt(jnp.mean(x * x, axis=-1, keepdims=True) + NORM_EPS)
    o_ref[...] = (y * g_ref[...]).astype(o_ref.dtype)


def rmsnorm(x, gain, tm=512):
    m, d = x.shape
    return pl.pallas_call(
        _rmsnorm_kernel,
        grid=(m // tm,),
        in_specs=[pl.BlockSpec((tm, d), lambda i: (i, 0)), pl.BlockSpec((1, d), lambda i: (0, 0))],
        out_specs=pl.BlockSpec((tm, d), lambda i: (i, 0)),
        out_shape=jax.ShapeDtypeStruct((m, d), BF16),
        compiler_params=_params("parallel"),
        name="rmsnorm",
    )(x, gain.reshape(1, d))


def _mm_kernel(*refs, act, has_bias, has_res, has_norm):
    x_ref, w_ref = refs[0], refs[1]
    acc = jnp.dot(x_ref[...], w_ref[...], preferred_element_type=F32)
    k = 2
    if has_bias:
        acc = acc + refs[k][...]
        k += 1
    if act == "tanh":
        acc = jnp.tanh(acc)
    elif act == "sigmoid":
        acc = jax.nn.sigmoid(acc)
    if has_res:
        acc = acc + refs[k][...]
        k += 1
    if has_norm:
        o_ref, n_ref = refs[-2], refs[-1]
        o_ref[...] = acc.astype(o_ref.dtype)
        y = acc * lax.rsqrt(jnp.mean(acc * acc, axis=-1, keepdims=True) + NORM_EPS)
        n_ref[...] = (y * refs[k][...]).astype(n_ref.dtype)
    else:
        refs[-1][...] = acc.astype(refs[-1].dtype)


def matmul(x, w, *, bias=None, act=None, residual=None, norm_gain=None, out_dtype=F32, tm=1024, tn=1024):
    m, kdim = x.shape
    n = w.shape[1]
    tm, tn = min(tm, m), min(tn, n)
    if norm_gain is not None:
        tn = n
    assert m % tm == 0 and n % tn == 0, (m, n, tm, tn)
    in_specs = [pl.BlockSpec((tm, kdim), lambda j, i: (i, 0)), pl.BlockSpec((kdim, tn), lambda j, i: (0, j))]
    args = [x, w]
    if bias is not None:
        in_specs.append(pl.BlockSpec((1, tn), lambda j, i: (0, j)))
        args.append(bias.reshape(1, n))
    if residual is not None:
        in_specs.append(pl.BlockSpec((tm, tn), lambda j, i: (i, j)))
        args.append(residual)
    out_spec = pl.BlockSpec((tm, tn), lambda j, i: (i, j))
    out_specs, out_shape = out_spec, jax.ShapeDtypeStruct((m, n), out_dtype)
    if norm_gain is not None:
        in_specs.append(pl.BlockSpec((1, tn), lambda j, i: (0, j)))
        args.append(norm_gain.reshape(1, n))
        out_specs, out_shape = [out_spec, out_spec], [out_shape, jax.ShapeDtypeStruct((m, n), BF16)]
    return pl.pallas_call(
        functools.partial(_mm_kernel, act=act, has_bias=bias is not None, has_res=residual is not None,
                          has_norm=norm_gain is not None),
        grid=(n // tn, m // tm),
        in_specs=in_specs,
        out_specs=out_specs,
        out_shape=out_shape,
        compiler_params=_params("parallel", "parallel"),
        name="matmul",
    )(*args)


def _ffn_up_kernel(x_ref, h_ref, wg_ref, wu_ref, cwg_ref, cwu_ref, cbg_ref, cbu_ref, o_ref, *, tiles_per_seq):
    i = pl.program_id(1)
    first = (i % tiles_per_seq) == 0
    x = x_ref[...]
    halo = h_ref[...]
    tm = x.shape[0]
    tn = o_ref.shape[1]
    row = lax.broadcasted_iota(jnp.int32, (tm, tn), 0)

    def causal_conv(w_ref, cw_ref, cb_ref):
        u = jnp.dot(x, w_ref[...], preferred_element_type=F32)
        uh = jnp.dot(halo, w_ref[...], preferred_element_type=F32)
        uh = jnp.where(first, 0.0, uh)
        u1 = jnp.where(row == 0, uh[7:8, :], pltpu.roll(u, 1, 0))
        u2 = pltpu.roll(u, 2, 0)
        u2 = jnp.where(row == 0, uh[6:7, :], jnp.where(row == 1, uh[7:8, :], u2))
        cw = cw_ref[...]
        return u2 * cw[0:1, :] + u1 * cw[1:2, :] + u * cw[2:3, :] + cb_ref[...]

    gate = causal_conv(wg_ref, cwg_ref, cbg_ref)
    up = causal_conv(wu_ref, cwu_ref, cbu_ref)
    o_ref[...] = (gate * jax.nn.sigmoid(gate) * up).astype(o_ref.dtype)


def ffn_up(h, w_up, conv_w, conv_b, seq, tm=1024, tn=512):
    m, d = h.shape
    f = w_up.shape[1] // 2
    nf = f // tn
    assert f % tn == 0 and seq % tm == 0
    hb = tm // SUBLANES
    return pl.pallas_call(
        functools.partial(_ffn_up_kernel, tiles_per_seq=seq // tm),
        grid=(nf, m // tm),
        in_specs=[
            pl.BlockSpec((tm, d), lambda c, i: (i, 0)),
            pl.BlockSpec((SUBLANES, d), lambda c, i: (jnp.maximum(i * hb - 1, 0), 0)),
            pl.BlockSpec((d, tn), lambda c, i: (0, c)),
            pl.BlockSpec((d, tn), lambda c, i: (0, c + nf)),
            pl.BlockSpec((CONV_WIDTH, tn), lambda c, i: (0, c)),
            pl.BlockSpec((CONV_WIDTH, tn), lambda c, i: (0, c + nf)),
            pl.BlockSpec((1, tn), lambda c, i: (0, c)),
            pl.BlockSpec((1, tn), lambda c, i: (0, c + nf)),
        ],
        out_specs=pl.BlockSpec((tm, tn), lambda c, i: (i, c)),
        out_shape=jax.ShapeDtypeStruct((m, f), BF16),
        compiler_params=_params("parallel", "parallel"),
        name="ffn_up_conv",
    )(h, h, w_up, w_up, conv_w, conv_w, conv_b.reshape(1, 2 * f), conv_b.reshape(1, 2 * f))


def conv_ffn(x, h, w_up, conv_w, conv_b, w_down, seq):
    act = ffn_up(h, w_up.astype(BF16), conv_w, conv_b, seq)
    return matmul(act, w_down.astype(BF16), residual=x, tm=512, tn=512)


def _rope_tables(seq):
    half = ROT_DIM // 2
    inv_freq = jnp.float32(ROPE_THETA) ** (-jnp.arange(0, ROT_DIM, 2, dtype=F32) / ROT_DIM)
    ang = jnp.arange(seq, dtype=F32)[:, None] * inv_freq[None, :]
    cos, sin = jnp.cos(ang), jnp.sin(ang)
    ones = jnp.ones((seq, MOBA_HEAD_DIM - ROT_DIM), F32)
    zeros_h = jnp.zeros((seq, half), F32)
    zeros_r = jnp.zeros((seq, MOBA_HEAD_DIM - ROT_DIM), F32)
    c_tab = jnp.concatenate([cos, cos, ones], axis=1)
    s_lo = jnp.concatenate([-sin, zeros_h, zeros_r], axis=1)
    s_hi = jnp.concatenate([zeros_h, sin, zeros_r], axis=1)
    return c_tab, s_lo, s_hi


def _moba_prep_kernel(qkv_ref, c_ref, slo_ref, shi_ref, qg_ref, kg_ref,
                      q_ref, k_ref, vt_ref, sel_ref, kmean_scr, *, blocks_per_seq):
    n = pl.program_id(0) % blocks_per_seq
    d = MOBA_HEADS * MOBA_HEAD_DIM
    half = ROT_DIM // 2
    c_tab, s_lo, s_hi = c_ref[...], slo_ref[...], shi_ref[...]

    @pl.when(n == 0)
    def _():
        kmean_scr[...] = jnp.zeros_like(kmean_scr)

    def norm_rope(x, gain):
        y = x * lax.rsqrt(jnp.mean(x * x, axis=-1, keepdims=True) + NORM_EPS) * gain
        return (y * c_tab + pltpu.roll(y, MOBA_HEAD_DIM - half, 1) * s_lo + pltpu.roll(y, half, 1) * s_hi)

    nb = sel_ref.shape[1]
    blk = lax.broadcasted_iota(jnp.int32, (nb, MOBA_BLOCK), 0)
    eligible = blk < n
    for h in range(MOBA_HEADS):
        lo = h * MOBA_HEAD_DIM
        q = norm_rope(qkv_ref[:, lo:lo + MOBA_HEAD_DIM], qg_ref[...])
        k = norm_rope(qkv_ref[:, d + lo:d + lo + MOBA_HEAD_DIM], kg_ref[...])
        q_ref[:, lo:lo + MOBA_HEAD_DIM] = (q * MOBA_Q_SCALE).astype(q_ref.dtype)
        k_ref[:, lo:lo + MOBA_HEAD_DIM] = k.astype(k_ref.dtype)
        vt_ref[lo:lo + MOBA_HEAD_DIM, :] = qkv_ref[:, 2 * d + lo:2 * d + lo + MOBA_HEAD_DIM].T.astype(vt_ref.dtype)
        gate = lax.dot_general(kmean_scr[:, lo:lo + MOBA_HEAD_DIM], q, (((1,), (1,)), ((), ())),
                               precision=HI, preferred_element_type=F32)
        gate = jnp.where(eligible, gate, -jnp.inf)
        sel = jnp.zeros((nb, MOBA_BLOCK), F32)
        for j in range(nb):
            gj = gate[j:j + 1, :]
            beats = jnp.where(gate > gj, 1.0, jnp.where((gate == gj) & (blk < j), 1.0, 0.0))
            rank = jnp.sum(beats, axis=0, keepdims=True)
            chosen = jnp.where((rank < MOBA_TOPK) & (j < n), 1.0, 0.0)
            sel = jnp.where(blk == j, chosen, sel)
        sel_ref[h] = sel
        km_rows = lax.broadcasted_iota(jnp.int32, (nb, MOBA_HEAD_DIM), 0)
        kmean_scr[:, lo:lo + MOBA_HEAD_DIM] = jnp.where(km_rows == n, jnp.mean(k, axis=0, keepdims=True),
                                                        kmean_scr[:, lo:lo + MOBA_HEAD_DIM])


def _moba_attn_kernel(q_ref, k_ref, vt_ref, sel_ref, o_ref):
    i = pl.program_id(2)
    heads = range(MOBA_HEADS_PER_CALL)
    lanes = [slice(h * MOBA_HEAD_DIM, (h + 1) * MOBA_HEAD_DIM) for h in heads]
    q = [q_ref[:, ln] for ln in lanes]
    nt_dims = (((1,), (1,)), ((), ()))

    def scores_t(start, h):
        return lax.dot_general(k_ref[pl.ds(start, MOBA_BLOCK), lanes[h]], q[h], nt_dims, preferred_element_type=F32)

    def weighted_values(start, h, p):
        return jnp.dot(vt_ref[lanes[h], pl.ds(start, MOBA_BLOCK)], p.astype(BF16), preferred_element_type=F32)

    own = pl.multiple_of(i * MOBA_BLOCK, MOBA_BLOCK)
    kpos = lax.broadcasted_iota(jnp.int32, (MOBA_BLOCK, MOBA_BLOCK), 0)
    qpos = lax.broadcasted_iota(jnp.int32, (MOBA_BLOCK, MOBA_BLOCK), 1)
    causal = kpos <= qpos
    s_own = [jnp.where(causal, scores_t(own, h), NEG_BIG) for h in heads]
    m0 = [jnp.max(x, axis=0, keepdims=True) for x in s_own]
    p0 = [jnp.exp2(s_own[h] - m0[h]) for h in heads]
    l0 = [jnp.sum(x, axis=0, keepdims=True) for x in p0]
    acc0 = [weighted_values(own, h, p0[h]) for h in heads]

    def body(j, carry):
        m_run, l_run, acc = carry
        start = pl.multiple_of(j * MOBA_BLOCK, MOBA_BLOCK)
        s_j = [jnp.where(sel_ref[h, pl.ds(j, 1), :] > 0.5, scores_t(start, h), NEG_BIG) for h in heads]
        m_new = [jnp.maximum(m_run[h], jnp.max(s_j[h], axis=0, keepdims=True)) for h in heads]
        alpha = [jnp.exp2(m_run[h] - m_new[h]) for h in heads]
        p_j = [jnp.exp2(s_j[h] - m_new[h]) for h in heads]
        l_new = [alpha[h] * l_run[h] + jnp.sum(p_j[h], axis=0, keepdims=True) for h in heads]
        acc = [alpha[h] * acc[h] + weighted_values(start, h, p_j[h]) for h in heads]
        return tuple(m_new), tuple(l_new), tuple(acc)

    _, l_fin, acc = lax.fori_loop(0, i, body, (tuple(m0), tuple(l0), tuple(acc0)))
    for h in heads:
        o_ref[:, lanes[h]] = (acc[h] / l_fin[h]).T.astype(o_ref.dtype)


def moba_layer(x, norm_g, wqkv, q_gain, k_gain, wo, next_norm, batch, seq):
    m, d = x.shape
    nb = seq // MOBA_BLOCK
    h = rmsnorm(x, norm_g)
    qkv = matmul(h, wqkv.astype(BF16))
    c_tab, s_lo, s_hi = _rope_tables(seq)
    tab_spec = pl.BlockSpec((MOBA_BLOCK, MOBA_HEAD_DIM), lambda i: (i % nb, 0))
    gain_spec = pl.BlockSpec((1, MOBA_HEAD_DIM), lambda i: (0, 0))
    q, k, vt, sel = pl.pallas_call(
        functools.partial(_moba_prep_kernel, blocks_per_seq=nb),
        grid=(m // MOBA_BLOCK,),
        in_specs=[pl.BlockSpec((MOBA_BLOCK, 3 * d), lambda i: (i, 0)), tab_spec, tab_spec, tab_spec,
                  gain_spec, gain_spec],
        out_specs=[pl.BlockSpec((MOBA_BLOCK, d), lambda i: (i, 0)),
                   pl.BlockSpec((MOBA_BLOCK, d), lambda i: (i, 0)),
                   pl.BlockSpec((d, MOBA_BLOCK), lambda i: (0, i)),
                   pl.BlockSpec((None, MOBA_HEADS, nb, MOBA_BLOCK), lambda i: (i, 0, 0, 0))],
        out_shape=[jax.ShapeDtypeStruct((m, d), BF16), jax.ShapeDtypeStruct((m, d), BF16),
                   jax.ShapeDtypeStruct((d, m), BF16),
                   jax.ShapeDtypeStruct((m // MOBA_BLOCK, MOBA_HEADS, nb, MOBA_BLOCK), F32)],
        scratch_shapes=[pltpu.VMEM((nb, d), F32)],
        compiler_params=_params("arbitrary"),
        name="moba_prep",
    )(qkv, c_tab, s_lo, s_hi, q_gain.reshape(1, -1), k_gain.reshape(1, -1))
    hw = MOBA_HEADS_PER_CALL * MOBA_HEAD_DIM
    attn = pl.pallas_call(
        _moba_attn_kernel,
        grid=(batch, MOBA_HEADS // MOBA_HEADS_PER_CALL, nb),
        in_specs=[pl.BlockSpec((MOBA_BLOCK, hw), lambda b, hh, i: (b * nb + i, hh)),
                  pl.BlockSpec((seq, hw), lambda b, hh, i: (b, hh)),
                  pl.BlockSpec((hw, seq), lambda b, hh, i: (hh, b)),
                  pl.BlockSpec((None, MOBA_HEADS_PER_CALL, nb, MOBA_BLOCK), lambda b, hh, i: (b * nb + i, hh, 0, 0))],
        out_specs=pl.BlockSpec((MOBA_BLOCK, hw), lambda b, hh, i: (b * nb + i, hh)),
        out_shape=jax.ShapeDtypeStruct((m, d), BF16),
        compiler_params=_params("parallel", "parallel", "parallel"),
        name="moba_attn",
    )(q, k, vt, sel)
    return matmul(attn, wo.astype(BF16), residual=x, norm_gain=next_norm, tm=512)


def _head_sum_matrix():
    r = lax.broadcasted_iota(jnp.int32, (LANES, LANES), 0) // RWKV_HEAD_SIZE
    c = lax.broadcasted_iota(jnp.int32, (LANES, LANES), 1) // RWKV_HEAD_SIZE
    return jnp.where(r == c, 1.0, 0.0).astype(BF16)


def _split_dot(x, y_exact, pieces):
    out = None
    for _ in range(pieces):
        part = x.astype(BF16)
        term = jnp.dot(part, y_exact, preferred_element_type=F32)
        out = term if out is None else out + term
        x = x - part.astype(F32)
    return out


def _head_sum(x, ones_bd):
    return _split_dot(x, ones_bd, 2)


def _rwkv_mix_kernel(x_ref, h_ref, g_ref, mu_ref, *o_refs, tiles_per_seq):
    first = (pl.program_id(0) % tiles_per_seq) == 0

    def norm(x):
        return (x * lax.rsqrt(jnp.mean(x * x, axis=-1, keepdims=True) + NORM_EPS) * g_ref[...])

    h = norm(x_ref[...])
    prev_row = jnp.where(first, 0.0, norm(h_ref[...])[SUBLANES - 1:SUBLANES, :])
    row = lax.broadcasted_iota(jnp.int32, h.shape, 0)
    xx = jnp.where(row == 0, prev_row, pltpu.roll(h, 1, 0)) - h
    for idx, o_ref in enumerate(o_refs):
        o_ref[...] = (h + xx * mu_ref[idx:idx + 1, :]).astype(o_ref.dtype)


def _rwkv_prep_kernel(k_ref, wp_ref, ap_ref, w0_ref, a0_ref, kk_ref, ka_ref,
                      logw_ref, kmod_ref, an_ref, b_ref):
    ones_bd = _head_sum_matrix()
    w_log = -jax.nn.softplus(-(w0_ref[...] + wp_ref[...])) - 0.5
    logw_ref[...] = -jnp.exp(w_log)
    a = jax.nn.sigmoid(a0_ref[...] + ap_ref[...])
    k = k_ref[...]
    kmod_ref[...] = k * (1.0 + (a - 1.0) * ka_ref[...])
    kk = k * kk_ref[...]
    for c in range(kk.shape[1] // LANES):
        sl = slice(c * LANES, (c + 1) * LANES)
        kc = kk[:, sl]
        nrm = jnp.maximum(jnp.sqrt(_head_sum(kc * kc, ones_bd)), 1e-12)
        kc = kc / nrm
        an_ref[:, sl] = -kc
        b_ref[:, sl] = kc * a[:, sl]


def _rwkv_chunk_kernel(r_ref, lw_ref, k_ref, v_ref, a_ref, b_ref, y_ref, s_scr):
    L, n, grp = RWKV_CHUNK, RWKV_HEAD_SIZE, RWKV_GROUP
    width = grp * n
    rows = grp * L
    tb = r_ref.shape[0]

    @pl.when(pl.program_id(2) == 0)
    def _():
        s_scr[...] = jnp.zeros_like(s_scr)

    row = lax.broadcasted_iota(jnp.int32, (rows, rows), 0)
    col = lax.broadcasted_iota(jnp.int32, (rows, rows), 1)
    strict = (col % L) < (row % L)
    incl = (col % L) <= (row % L)
    lane_head = lax.broadcasted_iota(jnp.int32, (L, width), 1) // n
    cum = jnp.where(lax.broadcasted_iota(jnp.int32, (L, L), 1) <= lax.broadcasted_iota(jnp.int32, (L, L), 0),
                    1.0, 0.0).astype(BF16)

    def cumsum_rows(x):
        out = None
        for _ in range(3):
            part = x.astype(BF16)
            term = jnp.dot(cum, part, preferred_element_type=F32)
            out = term if out is None else out + term
            x = x - part.astype(F32)
        return out
    nt_dims = (((1,), (1,)), ((), ()))
    tn_dims = (((0,), (0,)), ((), ()))

    def stack(x):
        return jnp.concatenate([jnp.where(lane_head == h, x, 0.0) for h in range(grp)], axis=0)

    def unstack(xm):
        out = xm[0:L]
        for h in range(1, grp):
            out = out + xm[h * L:(h + 1) * L]
        return out

    def mm(x, y):
        return jnp.dot(x.astype(BF16), y.astype(BF16), preferred_element_type=F32)

    n_groups = r_ref.shape[1] // width
    groups = range(n_groups)

    def chunk(c, carry):
        sl = pl.ds(pl.multiple_of(c * L, L), L)
        lanes = [slice(q * width, (q + 1) * width) for q in groups]
        lw = [lw_ref[sl, ln] for ln in lanes]
        g = [cumsum_rows(x) for x in lw]
        g_last = [x[L - 1:L, :] for x in g]
        e_neg = [jnp.exp(-x) for x in g]
        r, k, v, a, b = ([ref[sl, ln] for ln in lanes] for ref in (r_ref, k_ref, v_ref, a_ref, b_ref))
        lhs = [jnp.concatenate([stack(a[q] * jnp.exp(g[q] - lw[q])), stack(r[q] * jnp.exp(g[q]))], axis=0).astype(BF16)
               for q in groups]
        rhs = [jnp.concatenate([stack(b[q] * e_neg[q]), stack(k[q] * e_neg[q])], axis=0).astype(BF16) for q in groups]
        v_m = [stack(x) for x in v]
        s = [s_scr[q] for q in groups]
        prod = [lax.dot_general(lhs[q], rhs[q], nt_dims, preferred_element_type=F32) for q in groups]
        from_state = [lax.dot_general(lhs[q], s[q].astype(BF16), nt_dims, preferred_element_type=F32) for q in groups]
        a_ak = [jnp.where(strict, p[:rows, rows:], 0.0) for p in prod]
        u_m = [from_state[q][:rows] + mm(a_ak[q], v_m[q]) for q in groups]
        power = [jnp.where(strict, p[:rows, :rows], 0.0) for p in prod]
        for level in range(L.bit_length() - 1):
            if level:
                power = [mm(p, p) for p in power]
            u_m = [u_m[q] + mm(power[q], u_m[q]) for q in groups]
        for q in groups:
            m_both = jnp.where(jnp.concatenate([incl, incl], axis=1), prod[q][rows:, :], 0.0).astype(BF16)
            uv = jnp.concatenate([u_m[q], v_m[q]], axis=0).astype(BF16)
            y_ref[sl, lanes[q]] = unstack(from_state[q][rows:] + jnp.dot(m_both, uv, preferred_element_type=F32))
            e_tail = jnp.exp(g_last[q] - g[q])
            tail = jnp.concatenate([stack(b[q] * e_tail), stack(k[q] * e_tail)], axis=0).astype(BF16)
            s_scr[q] = s[q] * jnp.exp(g_last[q]) + lax.dot_general(uv, tail, tn_dims, preferred_element_type=F32)
        return carry

    lax.fori_loop(0, tb // L, chunk, 0)


def _rwkv_post_kernel(y_ref, r_ref, kmod_ref, v_ref, g_ref, rk_ref, lw_ref, lb_ref, o_ref):
    ones_bd = _head_sum_matrix()
    inv_n = 1.0 / RWKV_HEAD_SIZE
    for c in range(y_ref.shape[1] // LANES):
        sl = slice(c * LANES, (c + 1) * LANES)
        y = y_ref[:, sl]
        mean = _head_sum(y, ones_bd) * inv_n
        yc = y - mean
        var = _head_sum(yc * yc, ones_bd) * inv_n
        yn = yc * lax.rsqrt(var + RWKV_GN_EPS) * lw_ref[:, sl] + lb_ref[:, sl]
        bonus = _head_sum(r_ref[:, sl] * kmod_ref[:, sl] * rk_ref[:, sl], ones_bd) * v_ref[:, sl]
        o_ref[:, sl] = ((yn + bonus) * g_ref[:, sl]).astype(o_ref.dtype)


def rwkv_layer(x, norm_g, mu, w_r, w_k, w_v, w_o, w0, w1, w2, a0, a1, a2, g1, g2,
               k_k, k_a, r_k, lnx_w, lnx_b, next_norm, batch, seq):
    m, d = x.shape
    tm = 256
    row_spec = pl.BlockSpec((tm, d), lambda i: (i, 0))
    vec_spec = pl.BlockSpec((1, d), lambda i: (0, 0))
    hb = tm // SUBLANES
    mixed = pl.pallas_call(
        functools.partial(_rwkv_mix_kernel, tiles_per_seq=seq // tm),
        grid=(m // tm,),
        in_specs=[row_spec, pl.BlockSpec((SUBLANES, d), lambda i: (jnp.maximum(i * hb - 1, 0), 0)),
                  vec_spec, pl.BlockSpec((6, d), lambda i: (0, 0))],
        out_specs=[row_spec] * 6,
        out_shape=[jax.ShapeDtypeStruct((m, d), BF16)] * 6,
        compiler_params=_params("parallel"),
        name="rwkv_mix",
    )(x, x, norm_g.reshape(1, d), mu)
    x_r, x_w, x_k, x_v, x_a, x_g = mixed

    def lora_pad(w_in, w_out):
        rank = -(-w_in.shape[1] // LANES) * LANES
        return _pad_cols(w_in, rank).astype(BF16), _pad_rows(w_out, rank).astype(BF16)

    r = matmul(x_r, w_r.astype(BF16))
    k = matmul(x_k, w_k.astype(BF16))
    v = matmul(x_v, w_v.astype(BF16))
    w1p, w2p = lora_pad(w1, w2)
    a1p, a2p = lora_pad(a1, a2)
    g1p, g2p = lora_pad(g1, g2)
    w_pre = matmul(matmul(x_w, w1p, act="tanh", out_dtype=BF16), w2p)
    a_pre = matmul(matmul(x_a, a1p, out_dtype=BF16), a2p)
    gate = matmul(matmul(x_g, g1p, act="sigmoid", out_dtype=BF16), g2p)

    log_w, k_mod, a_neg, b_vec = pl.pallas_call(
        _rwkv_prep_kernel,
        grid=(m // tm,),
        in_specs=[row_spec] * 3 + [vec_spec] * 4,
        out_specs=[row_spec] * 4,
        out_shape=[jax.ShapeDtypeStruct((m, d), F32)] * 4,
        compiler_params=_params("parallel"),
        name="rwkv_prep",
    )(k, w_pre, a_pre, w0.reshape(1, d), a0.reshape(1, d), k_k.reshape(1, d), k_a.reshape(1, d))

    tb, groups_per_call = 256, 4
    width = groups_per_call * RWKV_GROUP * RWKV_HEAD_SIZE
    nt = seq // tb
    blk = pl.BlockSpec((tb, width), lambda b, p, t: (b * nt + t, p))
    y = pl.pallas_call(
        _rwkv_chunk_kernel,
        grid=(batch, d // width, nt),
        in_specs=[blk] * 6,
        out_specs=blk,
        out_shape=jax.ShapeDtypeStruct((m, d), F32),
        scratch_shapes=[pltpu.VMEM((groups_per_call, RWKV_GROUP * RWKV_HEAD_SIZE, RWKV_GROUP * RWKV_HEAD_SIZE), F32)],
        compiler_params=_params("parallel", "parallel", "arbitrary"),
        name="rwkv_chunks",
    )(r, log_w, k_mod, v, a_neg, b_vec)

    out = pl.pallas_call(
        _rwkv_post_kernel,
        grid=(m // tm,),
        in_specs=[row_spec] * 5 + [vec_spec] * 3,
        out_specs=row_spec,
        out_shape=jax.ShapeDtypeStruct((m, d), BF16),
        compiler_params=_params("parallel"),
        name="rwkv_post",
    )(y, r, k_mod, v, gate, r_k.reshape(1, d), lnx_w.reshape(1, d), lnx_b.reshape(1, d))
    return matmul(out, w_o.astype(BF16), residual=x, norm_gain=next_norm, tm=512)


def _softcap(z):
    return GATE_SOFTCAP * jnp.tanh(z / GATE_SOFTCAP)


def _mlstm_kernel(q_ref, k_ref, v_ref, o_ref, gc_ref, gr_ref, hg_ref, out_ref, ct_scr, n_scr, m_scr):
    L, H, dk, dv = MLSTM_CHUNK, MLSTM_HEADS, MLSTM_DQK, MLSTM_DV
    heads = range(H)

    @pl.when(pl.program_id(1) == 0)
    def _():
        ct_scr[...] = jnp.zeros_like(ct_scr)
        n_scr[...] = jnp.zeros_like(n_scr)
        m_scr[...] = jnp.zeros_like(m_scr)

    t_idx = lax.broadcasted_iota(jnp.int32, (L, L), 0)
    s_idx = lax.broadcasted_iota(jnp.int32, (L, L), 1)
    causal = s_idx <= t_idx
    anti = t_idx <= s_idx
    nt_dims = (((1,), (1,)), ((), ()))

    def chunk(c, carry):
        sl = pl.ds(pl.multiple_of(c * L, L), L)
        gc, gr = gc_ref[sl, :], gr_ref[c]
        li_cols, li_rows = _softcap(gc[:, :H]), _softcap(gr[:H, :])
        lf_cols = jax.nn.log_sigmoid(_softcap(gc[:, H:]))
        lf_rows = jax.nn.log_sigmoid(_softcap(gr[H:, :]))
        li_col = [li_cols[:, h:h + 1] for h in heads]
        li_row = [li_rows[h:h + 1, :] for h in heads]
        lf_col = [lf_cols[:, h:h + 1] for h in heads]
        lf_row = [lf_rows[h:h + 1, :] for h in heads]
        b_col = [jnp.sum(jnp.where(causal, lf_row[h], 0.0), axis=1, keepdims=True) for h in heads]
        b_row = [jnp.sum(jnp.where(anti, lf_col[h], 0.0), axis=0, keepdims=True) for h in heads]
        b_last = [jnp.sum(lf_row[h], axis=1, keepdims=True) for h in heads]
        m_prev = [m_scr[h] for h in heads]
        dmat = [jnp.where(causal, b_col[h] - b_row[h] + li_row[h], NEG_BIG) for h in heads]
        inter = [b_col[h] + m_prev[h] for h in heads]
        m_t = [jnp.maximum(inter[h], jnp.max(dmat[h], axis=1, keepdims=True)) for h in heads]
        q = [q_ref[sl, h * dk:(h + 1) * dk] for h in heads]
        k = [k_ref[sl, h * dk:(h + 1) * dk] * (dk ** -0.5) for h in heads]
        vb = [v_ref[sl, h * dv:(h + 1) * dv].astype(BF16) for h in heads]
        qb = [x.astype(BF16) for x in q]
        s = [lax.dot_general(qb[h], k[h].astype(BF16), nt_dims, preferred_element_type=F32) * jnp.exp(dmat[h] - m_t[h])
             for h in heads]
        w_inter = [jnp.exp(inter[h] - m_t[h]) for h in heads]
        ct = [ct_scr[h] for h in heads]
        n_row = [n_scr[h] for h in heads]
        num = [jnp.dot(s[h].astype(BF16), vb[h], preferred_element_type=F32)
               + w_inter[h] * jnp.dot(qb[h], ct[h].astype(BF16), preferred_element_type=F32) for h in heads]
        den = [jnp.sum(s[h], axis=1, keepdims=True) + w_inter[h] * jnp.sum(q[h] * n_row[h], axis=1, keepdims=True)
               for h in heads]
        h_c = [num[h] / jnp.maximum(jnp.abs(den[h]), jnp.exp(-m_t[h])) for h in heads]
        for h in heads:
            hn = h_c[h] * lax.rsqrt(jnp.mean(h_c[h] * h_c[h], axis=-1, keepdims=True) + NORM_EPS)
            cols = slice(h * dv, (h + 1) * dv)
            out_ref[sl, cols] = (hn * hg_ref[:, cols] * jax.nn.sigmoid(o_ref[sl, cols])).astype(out_ref.dtype)
        d_row = [b_last[h] - b_row[h] + li_row[h] for h in heads]
        m_new = [jnp.maximum(b_last[h] + m_prev[h], jnp.max(d_row[h], axis=1, keepdims=True)) for h in heads]
        kw = [k[h] * jnp.exp(b_last[h] - b_col[h] + li_col[h] - m_new[h]) for h in heads]
        w_c = [jnp.exp(b_last[h] + m_prev[h] - m_new[h]) for h in heads]
        for h in heads:
            ct_scr[h] = w_c[h] * ct[h] + jnp.dot(kw[h].T.astype(BF16), vb[h], preferred_element_type=F32)
            n_scr[h] = w_c[h] * n_row[h] + jnp.sum(kw[h], axis=0, keepdims=True)
            m_scr[h] = m_new[h]
        return carry

    lax.fori_loop(0, q_ref.shape[0] // L, chunk, 0)


def mlstm_layer(x, norm_g, w_in, b_if, head_gain, w_out, next_norm, batch, seq):
    m, d = x.shape
    H, L, dk, dv = MLSTM_HEADS, MLSTM_CHUNK, MLSTM_DQK, MLSTM_DV
    tb = 4 * L
    nt = seq // tb
    h = rmsnorm(x, norm_g)
    n_main = 2 * H * dk + 2 * H * dv
    w_in_b = w_in.astype(BF16)
    proj = matmul(h, w_in_b[:, :n_main])
    gates = matmul(h, _pad_cols(w_in_b[:, n_main:], LANES), bias=_pad_cols(b_if.reshape(1, 2 * H), LANES))
    gates = gates[:, :2 * H]
    gates_t = jnp.transpose(gates.reshape(m // L, L, 2 * H), (0, 2, 1))
    out = pl.pallas_call(
        _mlstm_kernel,
        grid=(batch, nt),
        in_specs=[pl.BlockSpec((tb, H * dk), lambda b, t: (b * nt + t, 0)),
                  pl.BlockSpec((tb, H * dk), lambda b, t: (b * nt + t, 1)),
                  pl.BlockSpec((tb, H * dv), lambda b, t: (b * nt + t, 1)),
                  pl.BlockSpec((tb, H * dv), lambda b, t: (b * nt + t, 2)),
                  pl.BlockSpec((tb, 2 * H), lambda b, t: (b * nt + t, 0)),
                  pl.BlockSpec((tb // L, 2 * H, L), lambda b, t: (b * nt + t, 0, 0)),
                  pl.BlockSpec((1, H * dv), lambda b, t: (0, 0))],
        out_specs=pl.BlockSpec((tb, H * dv), lambda b, t: (b * nt + t, 0)),
        out_shape=jax.ShapeDtypeStruct((m, H * dv), BF16),
        scratch_shapes=[pltpu.VMEM((H, dk, dv), F32), pltpu.VMEM((H, 1, dk), F32), pltpu.VMEM((H, 1, 1), F32)],
        compiler_params=_params("parallel", "arbitrary"),
        name="mlstm_chunks",
    )(proj, proj, proj, proj, gates, gates_t, head_gain.reshape(1, -1))
    return matmul(out, w_out.astype(BF16), residual=x, norm_gain=next_norm, tm=512)


def kernel(x, moba_norm, moba_wqkv, moba_q_gain, moba_k_gain, moba_wo, rwkv_norm, rwkv_mu, rwkv_w_r, rwkv_w_k, rwkv_w_v, rwkv_w_o, rwkv_w0, rwkv_w1, rwkv_w2, rwkv_a0, rwkv_a1, rwkv_a2, rwkv_g1, rwkv_g2, rwkv_k_k, rwkv_k_a, rwkv_r_k, rwkv_lnx_w, rwkv_lnx_b, mlstm_norm, mlstm_w_in, mlstm_b_if, mlstm_head_gain, mlstm_w_out, ffn_norm, ffn_w_up, ffn_conv_w, ffn_conv_b, ffn_w_down):
    batch, seq, d = x.shape
    depth = ffn_norm.shape[0]
    x = x.reshape(batch * seq, d)
    for i in range(depth):
        kind, j = i % 3, i // 3
        if kind == 0:
            x, h = moba_layer(x, moba_norm[j], moba_wqkv[j], moba_q_gain[j], moba_k_gain[j], moba_wo[j],
                              ffn_norm[i], batch, seq)
        elif kind == 1:
            x, h = rwkv_layer(x, rwkv_norm[j], rwkv_mu[j], rwkv_w_r[j], rwkv_w_k[j], rwkv_w_v[j], rwkv_w_o[j],
                           rwkv_w0[j], rwkv_w1[j], rwkv_w2[j], rwkv_a0[j], rwkv_a1[j], rwkv_a2[j],
                           rwkv_g1[j], rwkv_g2[j], rwkv_k_k[j], rwkv_k_a[j], rwkv_r_k[j],
                           rwkv_lnx_w[j], rwkv_lnx_b[j], ffn_norm[i], batch, seq)
        else:
            x, h = mlstm_layer(x, mlstm_norm[j], mlstm_w_in[j], mlstm_b_if[j], mlstm_head_gain[j],
                               mlstm_w_out[j], ffn_norm[i], batch, seq)
        x = conv_ffn(x, h, ffn_w_up[i], ffn_conv_w[i], ffn_conv_b[i], ffn_w_down[i], seq)
    return x.reshape(batch, seq, d)
```

```python
import functools

import jax
import jax.numpy as jnp
from jax import lax
from jax.experimental import pallas as pl
from jax.experimental.pallas import tpu as pltpu

F32 = jnp.float32
BF16 = jnp.bfloat16

LANES = 128
SUBLANES = 8
VMEM_LIMIT_BYTES = 56 * 1024 * 1024

NORM_EPS = 1e-6
MOBA_HEADS = 16
MOBA_HEAD_DIM = 128
MOBA_BLOCK = 256
MOBA_TOPK = 3
MOBA_HEADS_PER_CALL = 4
MOBA_VROWS = MOBA_HEAD_DIM + 16
MOBA_Q_SCALE = 1.4426950408889634 * MOBA_HEAD_DIM ** -0.5
ROPE_THETA = 500000.0
ROT_DIM = MOBA_HEAD_DIM // 4
RWKV_HEAD_SIZE = 64
RWKV_GN_EPS = 64e-5
RWKV_CHUNK = 64
RWKV_GROUP = 4
MLSTM_HEADS = 8
MLSTM_DV = 256
MLSTM_DQK = 128
MLSTM_CHUNK = 64
GATE_SOFTCAP = 15.0
CONV_WIDTH = 3
FFN_SUBTILE = 256
NEG_BIG = -1e30

HI = lax.Precision.HIGHEST


def _params(*semantics):
    return pltpu.CompilerParams(dimension_semantics=semantics, vmem_limit_bytes=VMEM_LIMIT_BYTES)


def _pad_cols(w, n):
    return jnp.pad(w, ((0, 0), (0, n - w.shape[1])))


def _pad_rows(w, n):
    return jnp.pad(w, ((0, n - w.shape[0]), (0, 0)))


def _rmsnorm_kernel(x_ref, g_ref, o_ref):
    x = x_ref[...]
    y = x * lax.rsqrt(jnp.mean(x * x, axis=-1, keepdims=True) + NORM_EPS)
    o_ref[...] = (y * g_ref[...]).astype(o_ref.dtype)


def rmsnorm(x, gain, tm=512):
    m, d = x.shape
    return pl.pallas_call(
        _rmsnorm_kernel,
        grid=(m // tm,),
        in_specs=[pl.BlockSpec((tm, d), lambda i: (i, 0)), pl.BlockSpec((1, d), lambda i: (0, 0))],
        out_specs=pl.BlockSpec((tm, d), lambda i: (i, 0)),
        out_shape=jax.ShapeDtypeStruct((m, d), BF16),
        compiler_params=_params("parallel"),
        name="rmsnorm",
    )(x, gain.reshape(1, d))


def _mm_kernel(*refs, act, has_bias, has_res, has_norm, cast_w):
    x_ref, w_ref = refs[0], refs[1]
    if cast_w:
        w_ref, refs = refs[-1], refs[:-1]

        @pl.when(pl.program_id(1) == 0)
        def _():
            w_ref[...] = refs[1][...].astype(BF16)

    acc = jnp.dot(x_ref[...], w_ref[...], preferred_element_type=F32)
    k = 2
    if has_bias:
        acc = acc + refs[k][...]
        k += 1
    if act == "tanh":
        acc = jnp.tanh(acc)
    elif act == "sigmoid":
        acc = jax.nn.sigmoid(acc)
    if has_res:
        acc = acc + refs[k][...]
        k += 1
    if has_norm:
        o_ref, n_ref = refs[-2], refs[-1]
        o_ref[...] = acc.astype(o_ref.dtype)
        y = acc * lax.rsqrt(jnp.mean(acc * acc, axis=-1, keepdims=True) + NORM_EPS)
        n_ref[...] = (y * refs[k][...]).astype(n_ref.dtype)
    else:
        refs[-1][...] = acc.astype(refs[-1].dtype)


def matmul(x, w, *, n=None, bias=None, act=None, residual=None, norm_gain=None, out_dtype=F32, tm=1024, tn=1024):
    m, kdim = x.shape
    n = w.shape[1] if n is None else n
    tm, tn = min(tm, m), min(tn, n)
    if norm_gain is not None:
        tn = n
    assert m % tm == 0 and n % tn == 0, (m, n, tm, tn)
    cast_w = w.dtype != BF16
    in_specs = [pl.BlockSpec((tm, kdim), lambda j, i: (i, 0)), pl.BlockSpec((kdim, tn), lambda j, i: (0, j))]
    args = [x, w]
    if bias is not None:
        in_specs.append(pl.BlockSpec((1, tn), lambda j, i: (0, j)))
        args.append(bias.reshape(1, n))
    if residual is not None:
        in_specs.append(pl.BlockSpec((tm, tn), lambda j, i: (i, j)))
        args.append(residual)
    out_spec = pl.BlockSpec((tm, tn), lambda j, i: (i, j))
    out_specs, out_shape = out_spec, jax.ShapeDtypeStruct((m, n), out_dtype)
    if norm_gain is not None:
        in_specs.append(pl.BlockSpec((1, tn), lambda j, i: (0, j)))
        args.append(norm_gain.reshape(1, n))
        out_specs, out_shape = [out_spec, out_spec], [out_shape, jax.ShapeDtypeStruct((m, n), BF16)]
    return pl.pallas_call(
        functools.partial(_mm_kernel, act=act, has_bias=bias is not None, has_res=residual is not None,
                          has_norm=norm_gain is not None, cast_w=cast_w),
        grid=(n // tn, m // tm),
        in_specs=in_specs,
        out_specs=out_specs,
        out_shape=out_shape,
        scratch_shapes=[pltpu.VMEM((kdim, tn), BF16)] if cast_w else [],
        compiler_params=_params("parallel", "arbitrary"),
        name="matmul",
    )(*args)


def _ffn_up_kernel(x_ref, h_ref, wg_ref, wu_ref, cwg_ref, cwu_ref, cbg_ref, cbu_ref, o_ref, wgb_scr, wub_scr,
                   *, tiles_per_seq):
    i = pl.program_id(1)
    first = (i % tiles_per_seq) == 0

    @pl.when(i == 0)
    def _():
        wgb_scr[...] = wg_ref[...].astype(BF16)
        wub_scr[...] = wu_ref[...].astype(BF16)

    x = x_ref[...]
    halo = h_ref[...]
    tm = x.shape[0]
    sub = FFN_SUBTILE
    row = lax.broadcasted_iota(jnp.int32, (tm, sub), 0)

    def causal_conv(w_scr, cw_ref, cb_ref, cols):
        u = jnp.dot(x, w_scr[:, cols], preferred_element_type=F32)
        uh = jnp.dot(halo, w_scr[:, cols], preferred_element_type=F32)
        uh = jnp.where(first, 0.0, uh)
        u1 = jnp.where(row == 0, uh[7:8, :], pltpu.roll(u, 1, 0))
        u2 = pltpu.roll(u, 2, 0)
        u2 = jnp.where(row == 0, uh[6:7, :], jnp.where(row == 1, uh[7:8, :], u2))
        cw = cw_ref[:, cols]
        return u2 * cw[0:1, :] + u1 * cw[1:2, :] + u * cw[2:3, :] + cb_ref[:, cols]

    for c in range(o_ref.shape[1] // sub):
        cols = slice(c * sub, (c + 1) * sub)
        gate = causal_conv(wgb_scr, cwg_ref, cbg_ref, cols)
        up = causal_conv(wub_scr, cwu_ref, cbu_ref, cols)
        o_ref[:, cols] = (gate * jax.nn.sigmoid(gate) * up).astype(o_ref.dtype)


def ffn_up(h, w_up, conv_w, conv_b, seq, tm=1024, tn=512):
    m, d = h.shape
    f = w_up.shape[1] // 2
    nf = f // tn
    assert f % tn == 0 and seq % tm == 0 and tn % FFN_SUBTILE == 0
    hb = tm // SUBLANES
    return pl.pallas_call(
        functools.partial(_ffn_up_kernel, tiles_per_seq=seq // tm),
        grid=(nf, m // tm),
        in_specs=[
            pl.BlockSpec((tm, d), lambda c, i: (i, 0)),
            pl.BlockSpec((SUBLANES, d), lambda c, i: (jnp.maximum(i * hb - 1, 0), 0)),
            pl.BlockSpec((d, tn), lambda c, i: (0, c)),
            pl.BlockSpec((d, tn), lambda c, i: (0, c + nf)),
            pl.BlockSpec((CONV_WIDTH, tn), lambda c, i: (0, c)),
            pl.BlockSpec((CONV_WIDTH, tn), lambda c, i: (0, c + nf)),
            pl.BlockSpec((1, tn), lambda c, i: (0, c)),
            pl.BlockSpec((1, tn), lambda c, i: (0, c + nf)),
        ],
        out_specs=pl.BlockSpec((tm, tn), lambda c, i: (i, c)),
        out_shape=jax.ShapeDtypeStruct((m, f), BF16),
        scratch_shapes=[pltpu.VMEM((d, tn), BF16), pltpu.VMEM((d, tn), BF16)],
        compiler_params=_params("parallel", "arbitrary"),
        name="ffn_up_conv",
    )(h, h, w_up, w_up, conv_w, conv_w, conv_b.reshape(1, 2 * f), conv_b.reshape(1, 2 * f))


def conv_ffn(x, h, w_up, conv_w, conv_b, w_down, seq):
    act = ffn_up(h, w_up, conv_w, conv_b, seq)
    return matmul(act, w_down, residual=x, tm=512, tn=512)


def _rope_tables(seq):
    half = ROT_DIM // 2
    inv_freq = jnp.float32(ROPE_THETA) ** (-jnp.arange(0, ROT_DIM, 2, dtype=F32) / ROT_DIM)
    ang = jnp.arange(seq, dtype=F32)[:, None] * inv_freq[None, :]
    cos, sin = jnp.cos(ang), jnp.sin(ang)
    ones = jnp.ones((seq, MOBA_HEAD_DIM - ROT_DIM), F32)
    zeros_h = jnp.zeros((seq, half), F32)
    zeros_r = jnp.zeros((seq, MOBA_HEAD_DIM - ROT_DIM), F32)
    c_tab = jnp.concatenate([cos, cos, ones], axis=1)
    s_lo = jnp.concatenate([-sin, zeros_h, zeros_r], axis=1)
    s_hi = jnp.concatenate([zeros_h, sin, zeros_r], axis=1)
    return c_tab, s_lo, s_hi


def _moba_prep_kernel(qkv_ref, c_ref, slo_ref, shi_ref, qg_ref, kg_ref,
                      q_ref, k_ref, vt_ref, sel_ref, kmean_scr, *, blocks_per_seq):
    n = pl.program_id(0) % blocks_per_seq
    d = MOBA_HEADS * MOBA_HEAD_DIM
    half = ROT_DIM // 2
    c_tab, s_lo, s_hi = c_ref[...], slo_ref[...], shi_ref[...]

    @pl.when(n == 0)
    def _():
        kmean_scr[...] = jnp.zeros_like(kmean_scr)

    def norm_rope(x, gain):
        y = x * lax.rsqrt(jnp.mean(x * x, axis=-1, keepdims=True) + NORM_EPS) * gain
        return (y * c_tab + pltpu.roll(y, MOBA_HEAD_DIM - half, 1) * s_lo + pltpu.roll(y, half, 1) * s_hi)

    nb = sel_ref.shape[1]
    blk = lax.broadcasted_iota(jnp.int32, (nb, MOBA_BLOCK), 0)
    eligible = blk < n
    for h in range(MOBA_HEADS):
        lo = h * MOBA_HEAD_DIM
        q = norm_rope(qkv_ref[:, lo:lo + MOBA_HEAD_DIM], qg_ref[...])
        k = norm_rope(qkv_ref[:, d + lo:d + lo + MOBA_HEAD_DIM], kg_ref[...])
        q_ref[:, lo:lo + MOBA_HEAD_DIM] = (q * MOBA_Q_SCALE).astype(q_ref.dtype)
        k_ref[:, lo:lo + MOBA_HEAD_DIM] = k.astype(k_ref.dtype)
        vlo = h * MOBA_VROWS
        vt_ref[vlo:vlo + MOBA_HEAD_DIM, :] = qkv_ref[:, 2 * d + lo:2 * d + lo + MOBA_HEAD_DIM].T.astype(vt_ref.dtype)
        vt_ref[vlo + MOBA_HEAD_DIM:vlo + MOBA_VROWS, :] = jnp.ones((MOBA_VROWS - MOBA_HEAD_DIM, MOBA_BLOCK), vt_ref.dtype)
        gate = lax.dot_general(kmean_scr[:, lo:lo + MOBA_HEAD_DIM], q, (((1,), (1,)), ((), ())),
                               precision=HI, preferred_element_type=F32)
        gate = jnp.where(eligible, gate, -jnp.inf)
        sel = jnp.zeros((nb, MOBA_BLOCK), F32)
        for j in range(nb):
            gj = gate[j:j + 1, :]
            beats = jnp.where(gate > gj, 1.0, jnp.where((gate == gj) & (blk < j), 1.0, 0.0))
            rank = jnp.sum(beats, axis=0, keepdims=True)
            chosen = jnp.where((rank < MOBA_TOPK) & (j < n), 1.0, 0.0)
            sel = jnp.where(blk == j, chosen, sel)
        sel_ref[h] = (sel - 1.0) * -NEG_BIG
        km_rows = lax.broadcasted_iota(jnp.int32, (nb, MOBA_HEAD_DIM), 0)
        kmean_scr[:, lo:lo + MOBA_HEAD_DIM] = jnp.where(km_rows == n, jnp.mean(k, axis=0, keepdims=True),
                                                        kmean_scr[:, lo:lo + MOBA_HEAD_DIM])


def _moba_attn_kernel(q_ref, k_ref, vt_ref, bias_ref, o_ref, s_scr):
    i = pl.program_id(2)
    heads = range(MOBA_HEADS_PER_CALL)
    lanes = [slice(h * MOBA_HEAD_DIM, (h + 1) * MOBA_HEAD_DIM) for h in heads]
    vrows = [slice(h * MOBA_VROWS, (h + 1) * MOBA_VROWS) for h in heads]
    q = [q_ref[:, ln] for ln in lanes]
    nt_dims = (((1,), (1,)), ((), ()))
    pair = 2 * MOBA_BLOCK
    n_pairs = (i + 2) // 2
    kpos = lax.broadcasted_iota(jnp.int32, (MOBA_BLOCK, MOBA_BLOCK), 0)
    qpos = lax.broadcasted_iota(jnp.int32, (MOBA_BLOCK, MOBA_BLOCK), 1)
    causal_bias = jnp.where(kpos <= qpos, 0.0, NEG_BIG)

    def store_scores(p, m_run):
        start = pl.multiple_of(p * pair, pair)
        m_new = []
        for h in heads:
            s = lax.dot_general(k_ref[pl.ds(start, pair), lanes[h]], q[h], nt_dims, preferred_element_type=F32)
            bias = jnp.concatenate([jnp.where(2 * p + e == i, causal_bias, bias_ref[h, pl.ds(2 * p + e, 1), :])
                                    for e in range(2)], axis=0)
            s = s + bias
            s_scr[h, pl.ds(start, pair), :] = s
            m_new.append(jnp.maximum(m_run[h], jnp.max(s.reshape(pair // SUBLANES, SUBLANES, MOBA_BLOCK), axis=0)))
        return tuple(m_new)

    m_part = lax.fori_loop(0, n_pairs, store_scores,
                           tuple(jnp.full((SUBLANES, MOBA_BLOCK), NEG_BIG, F32) for _ in heads))
    m_fin = [jnp.max(x, axis=0, keepdims=True) for x in m_part]

    def accumulate(p, acc):
        start = pl.multiple_of(p * pair, pair)
        return tuple(
            acc[h] + jnp.dot(vt_ref[vrows[h], pl.ds(start, pair)],
                             jnp.exp2(s_scr[h, pl.ds(start, pair), :] - m_fin[h]).astype(BF16),
                             preferred_element_type=F32)
            for h in heads)

    acc = lax.fori_loop(0, n_pairs, accumulate,
                        tuple(jnp.zeros((MOBA_VROWS, MOBA_BLOCK), F32) for _ in heads))
    for h in heads:
        out_t = acc[h][:MOBA_HEAD_DIM] / acc[h][MOBA_HEAD_DIM:MOBA_HEAD_DIM + 1]
        o_ref[:, lanes[h]] = out_t.T.astype(o_ref.dtype)


def moba_layer(x, norm_g, wqkv, q_gain, k_gain, wo, next_norm, batch, seq):
    m, d = x.shape
    nb = seq // MOBA_BLOCK
    h = rmsnorm(x, norm_g)
    qkv = matmul(h, wqkv)
    c_tab, s_lo, s_hi = _rope_tables(seq)
    tab_spec = pl.BlockSpec((MOBA_BLOCK, MOBA_HEAD_DIM), lambda i: (i % nb, 0))
    gain_spec = pl.BlockSpec((1, MOBA_HEAD_DIM), lambda i: (0, 0))
    q, k, vt, sel = pl.pallas_call(
        functools.partial(_moba_prep_kernel, blocks_per_seq=nb),
        grid=(m // MOBA_BLOCK,),
        in_specs=[pl.BlockSpec((MOBA_BLOCK, 3 * d), lambda i: (i, 0)), tab_spec, tab_spec, tab_spec,
                  gain_spec, gain_spec],
        out_specs=[pl.BlockSpec((MOBA_BLOCK, d), lambda i: (i, 0)),
                   pl.BlockSpec((MOBA_BLOCK, d), lambda i: (i, 0)),
                   pl.BlockSpec((MOBA_HEADS * MOBA_VROWS, MOBA_BLOCK), lambda i: (0, i)),
                   pl.BlockSpec((None, MOBA_HEADS, nb, MOBA_BLOCK), lambda i: (i, 0, 0, 0))],
        out_shape=[jax.ShapeDtypeStruct((m, d), BF16), jax.ShapeDtypeStruct((m, d), BF16),
                   jax.ShapeDtypeStruct((MOBA_HEADS * MOBA_VROWS, m), BF16),
                   jax.ShapeDtypeStruct((m // MOBA_BLOCK, MOBA_HEADS, nb, MOBA_BLOCK), F32)],
        scratch_shapes=[pltpu.VMEM((nb, d), F32)],
        compiler_params=_params("arbitrary"),
        name="moba_prep",
    )(qkv, c_tab, s_lo, s_hi, q_gain.reshape(1, -1), k_gain.reshape(1, -1))
    hw = MOBA_HEADS_PER_CALL * MOBA_HEAD_DIM
    attn = pl.pallas_call(
        _moba_attn_kernel,
        grid=(batch, MOBA_HEADS // MOBA_HEADS_PER_CALL, nb),
        in_specs=[pl.BlockSpec((MOBA_BLOCK, hw), lambda b, hh, i: (b * nb + i, hh)),
                  pl.BlockSpec((seq, hw), lambda b, hh, i: (b, hh)),
                  pl.BlockSpec((MOBA_HEADS_PER_CALL * MOBA_VROWS, seq), lambda b, hh, i: (hh, b)),
                  pl.BlockSpec((None, MOBA_HEADS_PER_CALL, nb, MOBA_BLOCK), lambda b, hh, i: (b * nb + i, hh, 0, 0))],
        out_specs=pl.BlockSpec((MOBA_BLOCK, hw), lambda b, hh, i: (b * nb + i, hh)),
        out_shape=jax.ShapeDtypeStruct((m, d), BF16),
        scratch_shapes=[pltpu.VMEM((MOBA_HEADS_PER_CALL, seq, MOBA_BLOCK), F32)],
        compiler_params=_params("parallel", "parallel", "parallel"),
        name="moba_attn",
    )(q, k, vt, sel)
    return matmul(attn, wo, residual=x, norm_gain=next_norm, tm=512)


def _head_sum_matrix():
    r = lax.broadcasted_iota(jnp.int32, (LANES, LANES), 0) // RWKV_HEAD_SIZE
    c = lax.broadcasted_iota(jnp.int32, (LANES, LANES), 1) // RWKV_HEAD_SIZE
    return jnp.where(r == c, 1.0, 0.0).astype(BF16)


def _split_dot(x, y_exact, pieces):
    out = None
    for _ in range(pieces):
        part = x.astype(BF16)
        term = jnp.dot(part, y_exact, preferred_element_type=F32)
        out = term if out is None else out + term
        x = x - part.astype(F32)
    return out


def _head_sum(x, ones_bd):
    return _split_dot(x, ones_bd, 2)


def _rwkv_mix_kernel(x_ref, h_ref, g_ref, mu_ref, *o_refs, tiles_per_seq):
    first = (pl.program_id(0) % tiles_per_seq) == 0

    def norm(x):
        return (x * lax.rsqrt(jnp.mean(x * x, axis=-1, keepdims=True) + NORM_EPS) * g_ref[...])

    h = norm(x_ref[...])
    prev_row = jnp.where(first, 0.0, norm(h_ref[...])[SUBLANES - 1:SUBLANES, :])
    row = lax.broadcasted_iota(jnp.int32, h.shape, 0)
    xx = jnp.where(row == 0, prev_row, pltpu.roll(h, 1, 0)) - h
    for idx, o_ref in enumerate(o_refs):
        o_ref[...] = (h + xx * mu_ref[idx:idx + 1, :]).astype(o_ref.dtype)


def _rwkv_prep_kernel(k_ref, wp_ref, ap_ref, w0_ref, a0_ref, kk_ref, ka_ref,
                      logw_ref, kmod_ref, an_ref, b_ref):
    ones_bd = _head_sum_matrix()
    w_log = -jax.nn.softplus(-(w0_ref[...] + wp_ref[...])) - 0.5
    logw_ref[...] = -jnp.exp(w_log)
    a = jax.nn.sigmoid(a0_ref[...] + ap_ref[...])
    k = k_ref[...]
    kmod_ref[...] = k * (1.0 + (a - 1.0) * ka_ref[...])
    kk = k * kk_ref[...]
    for c in range(kk.shape[1] // LANES):
        sl = slice(c * LANES, (c + 1) * LANES)
        kc = kk[:, sl]
        nrm = jnp.maximum(jnp.sqrt(_head_sum(kc * kc, ones_bd)), 1e-12)
        kc = kc / nrm
        an_ref[:, sl] = -kc
        b_ref[:, sl] = kc * a[:, sl]


def _rwkv_chunk_kernel(r_ref, lw_ref, k_ref, v_ref, a_ref, b_ref, y_ref, s_scr):
    L, n, grp = RWKV_CHUNK, RWKV_HEAD_SIZE, RWKV_GROUP
    width = grp * n
    rows = grp * L
    tb = r_ref.shape[0]

    @pl.when(pl.program_id(2) == 0)
    def _():
        s_scr[...] = jnp.zeros_like(s_scr)

    row = lax.broadcasted_iota(jnp.int32, (rows, rows), 0)
    col = lax.broadcasted_iota(jnp.int32, (rows, rows), 1)
    strict = (col % L) < (row % L)
    incl = (col % L) <= (row % L)
    lane_head = lax.broadcasted_iota(jnp.int32, (L, width), 1) // n
    cum = jnp.where(lax.broadcasted_iota(jnp.int32, (L, L), 1) <= lax.broadcasted_iota(jnp.int32, (L, L), 0),
                    1.0, 0.0).astype(BF16)

    def cumsum_rows(x):
        out = None
        for _ in range(3):
            part = x.astype(BF16)
            term = jnp.dot(cum, part, preferred_element_type=F32)
            out = term if out is None else out + term
            x = x - part.astype(F32)
        return out
    nt_dims = (((1,), (1,)), ((), ()))
    tn_dims = (((0,), (0,)), ((), ()))

    def stack(x):
        return jnp.concatenate([jnp.where(lane_head == h, x, 0.0) for h in range(grp)], axis=0)

    def unstack(xm):
        out = xm[0:L]
        for h in range(1, grp):
            out = out + xm[h * L:(h + 1) * L]
        return out

    def mm(x, y):
        return jnp.dot(x.astype(BF16), y.astype(BF16), preferred_element_type=F32)

    n_groups = r_ref.shape[1] // width
    groups = range(n_groups)

    def chunk(c, carry):
        sl = pl.ds(pl.multiple_of(c * L, L), L)
        lanes = [slice(q * width, (q + 1) * width) for q in groups]
        lw = [lw_ref[sl, ln] for ln in lanes]
        g = [cumsum_rows(x) for x in lw]
        g_last = [x[L - 1:L, :] for x in g]
        e_neg = [jnp.exp(-x) for x in g]
        r, k, v, a, b = ([ref[sl, ln] for ln in lanes] for ref in (r_ref, k_ref, v_ref, a_ref, b_ref))
        lhs = [jnp.concatenate([stack(a[q] * jnp.exp(g[q] - lw[q])), stack(r[q] * jnp.exp(g[q]))], axis=0).astype(BF16)
               for q in groups]
        rhs = [jnp.concatenate([stack(b[q] * e_neg[q]), stack(k[q] * e_neg[q])], axis=0).astype(BF16) for q in groups]
        v_m = [stack(x) for x in v]
        s = [s_scr[q] for q in groups]
        prod = [lax.dot_general(lhs[q], rhs[q], nt_dims, preferred_element_type=F32) for q in groups]
        from_state = [lax.dot_general(lhs[q], s[q].astype(BF16), nt_dims, preferred_element_type=F32) for q in groups]
        a_ak = [jnp.where(strict, p[:rows, rows:], 0.0) for p in prod]
        u_m = [from_state[q][:rows] + mm(a_ak[q], v_m[q]) for q in groups]
        power = [jnp.where(strict, p[:rows, :rows], 0.0) for p in prod]
        for level in range(L.bit_length() - 1):
            if level:
                power = [mm(p, p) for p in power]
            u_m = [u_m[q] + mm(power[q], u_m[q]) for q in groups]
        for q in groups:
            m_both = jnp.where(jnp.concatenate([incl, incl], axis=1), prod[q][rows:, :], 0.0).astype(BF16)
            uv = jnp.concatenate([u_m[q], v_m[q]], axis=0).astype(BF16)
            y_ref[sl, lanes[q]] = unstack(from_state[q][rows:] + jnp.dot(m_both, uv, preferred_element_type=F32))
            e_tail = jnp.exp(g_last[q] - g[q])
            tail = jnp.concatenate([stack(b[q] * e_tail), stack(k[q] * e_tail)], axis=0).astype(BF16)
            s_scr[q] = s[q] * jnp.exp(g_last[q]) + lax.dot_general(uv, tail, tn_dims, preferred_element_type=F32)
        return carry

    lax.fori_loop(0, tb // L, chunk, 0)


def _rwkv_post_kernel(y_ref, r_ref, kmod_ref, v_ref, g_ref, rk_ref, lw_ref, lb_ref, o_ref):
    ones_bd = _head_sum_matrix()
    inv_n = 1.0 / RWKV_HEAD_SIZE
    for c in range(y_ref.shape[1] // LANES):
        sl = slice(c * LANES, (c + 1) * LANES)
        y = y_ref[:, sl]
        mean = _head_sum(y, ones_bd) * inv_n
        yc = y - mean
        var = _head_sum(yc * yc, ones_bd) * inv_n
        yn = yc * lax.rsqrt(var + RWKV_GN_EPS) * lw_ref[:, sl] + lb_ref[:, sl]
        bonus = _head_sum(r_ref[:, sl] * kmod_ref[:, sl] * rk_ref[:, sl], ones_bd) * v_ref[:, sl]
        o_ref[:, sl] = ((yn + bonus) * g_ref[:, sl]).astype(o_ref.dtype)


def rwkv_layer(x, norm_g, mu, w_r, w_k, w_v, w_o, w0, w1, w2, a0, a1, a2, g1, g2,
               k_k, k_a, r_k, lnx_w, lnx_b, next_norm, batch, seq):
    m, d = x.shape
    tm = 256
    row_spec = pl.BlockSpec((tm, d), lambda i: (i, 0))
    vec_spec = pl.BlockSpec((1, d), lambda i: (0, 0))
    hb = tm // SUBLANES
    mixed = pl.pallas_call(
        functools.partial(_rwkv_mix_kernel, tiles_per_seq=seq // tm),
        grid=(m // tm,),
        in_specs=[row_spec, pl.BlockSpec((SUBLANES, d), lambda i: (jnp.maximum(i * hb - 1, 0), 0)),
                  vec_spec, pl.BlockSpec((6, d), lambda i: (0, 0))],
        out_specs=[row_spec] * 6,
        out_shape=[jax.ShapeDtypeStruct((m, d), BF16)] * 6,
        compiler_params=_params("parallel"),
        name="rwkv_mix",
    )(x, x, norm_g.reshape(1, d), mu)
    x_r, x_w, x_k, x_v, x_a, x_g = mixed

    def lora_pad(w_in, w_out):
        rank = -(-w_in.shape[1] // LANES) * LANES
        return _pad_cols(w_in, rank), _pad_rows(w_out, rank)

    r = matmul(x_r, w_r)
    k = matmul(x_k, w_k)
    v = matmul(x_v, w_v)
    w1p, w2p = lora_pad(w1, w2)
    a1p, a2p = lora_pad(a1, a2)
    g1p, g2p = lora_pad(g1, g2)
    w_pre = matmul(matmul(x_w, w1p, act="tanh", out_dtype=BF16), w2p)
    a_pre = matmul(matmul(x_a, a1p, out_dtype=BF16), a2p)
    gate = matmul(matmul(x_g, g1p, act="sigmoid", out_dtype=BF16), g2p)

    log_w, k_mod, a_neg, b_vec = pl.pallas_call(
        _rwkv_prep_kernel,
        grid=(m // tm,),
        in_specs=[row_spec] * 3 + [vec_spec] * 4,
        out_specs=[row_spec] * 4,
        out_shape=[jax.ShapeDtypeStruct((m, d), F32)] * 4,
        compiler_params=_params("parallel"),
        name="rwkv_prep",
    )(k, w_pre, a_pre, w0.reshape(1, d), a0.reshape(1, d), k_k.reshape(1, d), k_a.reshape(1, d))

    tb, groups_per_call = 256, 4
    width = groups_per_call * RWKV_GROUP * RWKV_HEAD_SIZE
    nt = seq // tb
    blk = pl.BlockSpec((tb, width), lambda b, p, t: (b * nt + t, p))
    y = pl.pallas_call(
        _rwkv_chunk_kernel,
        grid=(batch, d // width, nt),
        in_specs=[blk] * 6,
        out_specs=blk,
        out_shape=jax.ShapeDtypeStruct((m, d), F32),
        scratch_shapes=[pltpu.VMEM((groups_per_call, RWKV_GROUP * RWKV_HEAD_SIZE, RWKV_GROUP * RWKV_HEAD_SIZE), F32)],
        compiler_params=_params("parallel", "parallel", "arbitrary"),
        name="rwkv_chunks",
    )(r, log_w, k_mod, v, a_neg, b_vec)

    out = pl.pallas_call(
        _rwkv_post_kernel,
        grid=(m // tm,),
        in_specs=[row_spec] * 5 + [vec_spec] * 3,
        out_specs=row_spec,
        out_shape=jax.ShapeDtypeStruct((m, d), BF16),
        compiler_params=_params("parallel"),
        name="rwkv_post",
    )(y, r, k_mod, v, gate, r_k.reshape(1, d), lnx_w.reshape(1, d), lnx_b.reshape(1, d))
    return matmul(out, w_o, residual=x, norm_gain=next_norm, tm=512)


def _softcap(z):
    return GATE_SOFTCAP * jnp.tanh(z / GATE_SOFTCAP)


def _mlstm_kernel(q_ref, k_ref, v_ref, o_ref, gc_ref, gr_ref, hg_ref, out_ref, ct_scr, n_scr, m_scr):
    L, H, dk, dv = MLSTM_CHUNK, MLSTM_HEADS, MLSTM_DQK, MLSTM_DV
    heads = range(H)

    @pl.when(pl.program_id(1) == 0)
    def _():
        ct_scr[...] = jnp.zeros_like(ct_scr)
        n_scr[...] = jnp.zeros_like(n_scr)
        m_scr[...] = jnp.zeros_like(m_scr)

    t_idx = lax.broadcasted_iota(jnp.int32, (L, L), 0)
    s_idx = lax.broadcasted_iota(jnp.int32, (L, L), 1)
    causal = s_idx <= t_idx
    anti = t_idx <= s_idx
    nt_dims = (((1,), (1,)), ((), ()))

    def chunk(c, carry):
        sl = pl.ds(pl.multiple_of(c * L, L), L)
        gc, gr = gc_ref[sl, :], gr_ref[c]
        li_cols, li_rows = _softcap(gc[:, :H]), _softcap(gr[:H, :])
        lf_cols = jax.nn.log_sigmoid(_softcap(gc[:, H:]))
        lf_rows = jax.nn.log_sigmoid(_softcap(gr[H:, :]))
        li_col = [li_cols[:, h:h + 1] for h in heads]
        li_row = [li_rows[h:h + 1, :] for h in heads]
        lf_col = [lf_cols[:, h:h + 1] for h in heads]
        lf_row = [lf_rows[h:h + 1, :] for h in heads]
        b_col = [jnp.sum(jnp.where(causal, lf_row[h], 0.0), axis=1, keepdims=True) for h in heads]
        b_row = [jnp.sum(jnp.where(anti, lf_col[h], 0.0), axis=0, keepdims=True) for h in heads]
        b_last = [jnp.sum(lf_row[h], axis=1, keepdims=True) for h in heads]
        m_prev = [m_scr[h] for h in heads]
        dmat = [jnp.where(causal, b_col[h] - b_row[h] + li_row[h], NEG_BIG) for h in heads]
        inter = [b_col[h] + m_prev[h] for h in heads]
        m_t = [jnp.maximum(inter[h], jnp.max(dmat[h], axis=1, keepdims=True)) for h in heads]
        q = [q_ref[sl, h * dk:(h + 1) * dk] for h in heads]
        k = [k_ref[sl, h * dk:(h + 1) * dk] * (dk ** -0.5) for h in heads]
        vb = [v_ref[sl, h * dv:(h + 1) * dv].astype(BF16) for h in heads]
        qb = [x.astype(BF16) for x in q]
        s = [lax.dot_general(qb[h], k[h].astype(BF16), nt_dims, preferred_element_type=F32) * jnp.exp(dmat[h] - m_t[h])
             for h in heads]
        w_inter = [jnp.exp(inter[h] - m_t[h]) for h in heads]
        ct = [ct_scr[h] for h in heads]
        n_row = [n_scr[h] for h in heads]
        num = [jnp.dot(s[h].astype(BF16), vb[h], preferred_element_type=F32)
               + w_inter[h] * jnp.dot(qb[h], ct[h].astype(BF16), preferred_element_type=F32) for h in heads]
        den = [jnp.sum(s[h], axis=1, keepdims=True) + w_inter[h] * jnp.sum(q[h] * n_row[h], axis=1, keepdims=True)
               for h in heads]
        h_c = [num[h] / jnp.maximum(jnp.abs(den[h]), jnp.exp(-m_t[h])) for h in heads]
        for h in heads:
            hn = h_c[h] * lax.rsqrt(jnp.mean(h_c[h] * h_c[h], axis=-1, keepdims=True) + NORM_EPS)
            cols = slice(h * dv, (h + 1) * dv)
            out_ref[sl, cols] = (hn * hg_ref[:, cols] * jax.nn.sigmoid(o_ref[sl, cols])).astype(out_ref.dtype)
        d_row = [b_last[h] - b_row[h] + li_row[h] for h in heads]
        m_new = [jnp.maximum(b_last[h] + m_prev[h], jnp.max(d_row[h], axis=1, keepdims=True)) for h in heads]
        kw = [k[h] * jnp.exp(b_last[h] - b_col[h] + li_col[h] - m_new[h]) for h in heads]
        w_c = [jnp.exp(b_last[h] + m_prev[h] - m_new[h]) for h in heads]
        for h in heads:
            ct_scr[h] = w_c[h] * ct[h] + jnp.dot(kw[h].T.astype(BF16), vb[h], preferred_element_type=F32)
            n_scr[h] = w_c[h] * n_row[h] + jnp.sum(kw[h], axis=0, keepdims=True)
            m_scr[h] = m_new[h]
        return carry

    lax.fori_loop(0, q_ref.shape[0] // L, chunk, 0)


def mlstm_layer(x, norm_g, w_in, b_if, head_gain, w_out, next_norm, batch, seq):
    m, d = x.shape
    H, L, dk, dv = MLSTM_HEADS, MLSTM_CHUNK, MLSTM_DQK, MLSTM_DV
    tb = 4 * L
    nt = seq // tb
    h = rmsnorm(x, norm_g)
    n_main = 2 * H * dk + 2 * H * dv
    proj = matmul(h, w_in, n=n_main)
    gates = matmul(h, _pad_cols(w_in[:, n_main:], LANES), bias=_pad_cols(b_if.reshape(1, 2 * H), LANES))
    gates = gates[:, :2 * H]
    gates_t = jnp.transpose(gates.reshape(m // L, L, 2 * H), (0, 2, 1))
    out = pl.pallas_call(
        _mlstm_kernel,
        grid=(batch, nt),
        in_specs=[pl.BlockSpec((tb, H * dk), lambda b, t: (b * nt + t, 0)),
                  pl.BlockSpec((tb, H * dk), lambda b, t: (b * nt + t, 1)),
                  pl.BlockSpec((tb, H * dv), lambda b, t: (b * nt + t, 1)),
                  pl.BlockSpec((tb, H * dv), lambda b, t: (b * nt + t, 2)),
                  pl.BlockSpec((tb, 2 * H), lambda b, t: (b * nt + t, 0)),
                  pl.BlockSpec((tb // L, 2 * H, L), lambda b, t: (b * nt + t, 0, 0)),
                  pl.BlockSpec((1, H * dv), lambda b, t: (0, 0))],
        out_specs=pl.BlockSpec((tb, H * dv), lambda b, t: (b * nt + t, 0)),
        out_shape=jax.ShapeDtypeStruct((m, H * dv), BF16),
        scratch_shapes=[pltpu.VMEM((H, dk, dv), F32), pltpu.VMEM((H, 1, dk), F32), pltpu.VMEM((H, 1, 1), F32)],
        compiler_params=_params("parallel", "arbitrary"),
        name="mlstm_chunks",
    )(proj, proj, proj, proj, gates, gates_t, head_gain.reshape(1, -1))
    return matmul(out, w_out, residual=x, norm_gain=next_norm, tm=512)


def kernel(x, moba_norm, moba_wqkv, moba_q_gain, moba_k_gain, moba_wo, rwkv_norm, rwkv_mu, rwkv_w_r, rwkv_w_k, rwkv_w_v, rwkv_w_o, rwkv_w0, rwkv_w1, rwkv_w2, rwkv_a0, rwkv_a1, rwkv_a2, rwkv_g1, rwkv_g2, rwkv_k_k, rwkv_k_a, rwkv_r_k, rwkv_lnx_w, rwkv_lnx_b, mlstm_norm, mlstm_w_in, mlstm_b_if, mlstm_head_gain, mlstm_w_out, ffn_norm, ffn_w_up, ffn_conv_w, ffn_conv_b, ffn_w_down):
    batch, seq, d = x.shape
    depth = ffn_norm.shape[0]
    x = x.reshape(batch * seq, d)
    for i in range(depth):
        kind, j = i % 3, i // 3
        if kind == 0:
            x, h = moba_layer(x, moba_norm[j], moba_wqkv[j], moba_q_gain[j], moba_k_gain[j], moba_wo[j],
                              ffn_norm[i], batch, seq)
        elif kind == 1:
            x, h = rwkv_layer(x, rwkv_norm[j], rwkv_mu[j], rwkv_w_r[j], rwkv_w_k[j], rwkv_w_v[j], rwkv_w_o[j],
                           rwkv_w0[j], rwkv_w1[j], rwkv_w2[j], rwkv_a0[j], rwkv_a1[j], rwkv_a2[j],
                           rwkv_g1[j], rwkv_g2[j], rwkv_k_k[j], rwkv_k_a[j], rwkv_r_k[j],
                           rwkv_lnx_w[j], rwkv_lnx_b[j], ffn_norm[i], batch, seq)
        else:
            x, h = mlstm_layer(x, mlstm_norm[j], mlstm_w_in[j], mlstm_b_if[j], mlstm_head_gain[j],
                               mlstm_w_out[j], ffn_norm[i], batch, seq)
        x = conv_ffn(x, h, ffn_w_up[i], ffn_conv_w[i], ffn_conv_b[i], ffn_w_down[i], seq)
    return x.reshape(batch, seq, d)
```

```python
import functools

import jax
import jax.numpy as jnp
from jax import lax
from jax.experimental import pallas as pl
from jax.experimental.pallas import tpu as pltpu

F32 = jnp.float32
BF16 = jnp.bfloat16

LANES = 128
SUBLANES = 8
VMEM_LIMIT_BYTES = 56 * 1024 * 1024

NORM_EPS = 1e-6
MOBA_HEADS = 16
MOBA_HEAD_DIM = 128
MOBA_BLOCK = 256
MOBA_TOPK = 3
MOBA_HEADS_PER_CALL = 4
MOBA_VROWS = MOBA_HEAD_DIM + 16
MOBA_Q_SCALE = 1.4426950408889634 * MOBA_HEAD_DIM ** -0.5
ROPE_THETA = 500000.0
ROT_DIM = MOBA_HEAD_DIM // 4
RWKV_HEAD_SIZE = 64
RWKV_GN_EPS = 64e-5
RWKV_CHUNK = 64
RWKV_GROUP = 4
MLSTM_HEADS = 8
MLSTM_DV = 256
MLSTM_DQK = 128
MLSTM_CHUNK = 64
GATE_SOFTCAP = 15.0
CONV_WIDTH = 3
FFN_HALO = 16
NEG_BIG = -1e30

HI = lax.Precision.HIGHEST


def _params(*semantics):
    return pltpu.CompilerParams(dimension_semantics=semantics, vmem_limit_bytes=VMEM_LIMIT_BYTES)


def _pad_cols(w, n):
    return jnp.pad(w, ((0, 0), (0, n - w.shape[1])))


def _pad_rows(w, n):
    return jnp.pad(w, ((0, n - w.shape[0]), (0, 0)))


def _rmsnorm_kernel(x_ref, g_ref, o_ref):
    x = x_ref[...]
    y = x * lax.rsqrt(jnp.mean(x * x, axis=-1, keepdims=True) + NORM_EPS)
    o_ref[...] = (y * g_ref[...]).astype(o_ref.dtype)


def rmsnorm(x, gain, tm=512):
    m, d = x.shape
    return pl.pallas_call(
        _rmsnorm_kernel,
        grid=(m // tm,),
        in_specs=[pl.BlockSpec((tm, d), lambda i: (i, 0)), pl.BlockSpec((1, d), lambda i: (0, 0))],
        out_specs=pl.BlockSpec((tm, d), lambda i: (i, 0)),
        out_shape=jax.ShapeDtypeStruct((m, d), BF16),
        compiler_params=_params("parallel"),
        name="rmsnorm",
    )(x, gain.reshape(1, d))


def _mm_kernel(*refs, act, has_bias, has_res, has_norm, cast_w):
    x_ref, w_ref = refs[0], refs[1]
    if cast_w:
        w_ref, refs = refs[-1], refs[:-1]

        @pl.when(pl.program_id(1) == 0)
        def _():
            w_ref[...] = refs[1][...].astype(BF16)

    acc = jnp.dot(x_ref[...], w_ref[...], preferred_element_type=F32)
    k = 2
    if has_bias:
        acc = acc + refs[k][...]
        k += 1
    if act == "tanh":
        acc = jnp.tanh(acc)
    elif act == "sigmoid":
        acc = jax.nn.sigmoid(acc)
    if has_res:
        acc = acc + refs[k][...]
        k += 1
    if has_norm:
        o_ref, n_ref = refs[-2], refs[-1]
        o_ref[...] = acc.astype(o_ref.dtype)
        y = acc * lax.rsqrt(jnp.mean(acc * acc, axis=-1, keepdims=True) + NORM_EPS)
        n_ref[...] = (y * refs[k][...]).astype(n_ref.dtype)
    else:
        refs[-1][...] = acc.astype(refs[-1].dtype)


def matmul(x, w, *, layer=None, n=None, bias=None, act=None, residual=None, norm_gain=None, out_dtype=F32,
           tm=1024, tn=1024):
    m, kdim = x.shape
    n = w.shape[-1] if n is None else n
    tm, tn = min(tm, m), min(tn, n)
    if norm_gain is not None:
        tn = n
    assert m % tm == 0 and n % tn == 0, (m, n, tm, tn)
    cast_w = w.dtype != BF16
    if layer is None:
        w_spec = pl.BlockSpec((kdim, tn), lambda j, i: (0, j))
    else:
        w_spec = pl.BlockSpec((None, kdim, tn), lambda j, i: (layer, 0, j))
    in_specs = [pl.BlockSpec((tm, kdim), lambda j, i: (i, 0)), w_spec]
    args = [x, w]
    if bias is not None:
        in_specs.append(pl.BlockSpec((1, tn), lambda j, i: (0, j)))
        args.append(bias.reshape(1, n))
    if residual is not None:
        in_specs.append(pl.BlockSpec((tm, tn), lambda j, i: (i, j)))
        args.append(residual)
    out_spec = pl.BlockSpec((tm, tn), lambda j, i: (i, j))
    out_specs, out_shape = out_spec, jax.ShapeDtypeStruct((m, n), out_dtype)
    if norm_gain is not None:
        in_specs.append(pl.BlockSpec((1, tn), lambda j, i: (0, j)))
        args.append(norm_gain.reshape(1, n))
        out_specs, out_shape = [out_spec, out_spec], [out_shape, jax.ShapeDtypeStruct((m, n), BF16)]
    return pl.pallas_call(
        functools.partial(_mm_kernel, act=act, has_bias=bias is not None, has_res=residual is not None,
                          has_norm=norm_gain is not None, cast_w=cast_w),
        grid=(n // tn, m // tm),
        in_specs=in_specs,
        out_specs=out_specs,
        out_shape=out_shape,
        scratch_shapes=[pltpu.VMEM((kdim, tn), BF16)] if cast_w else [],
        compiler_params=_params("parallel", "arbitrary"),
        name="matmul",
    )(*args)


def _ffn_up_kernel(x_ref, h_ref, wg_ref, wu_ref, cwg_ref, cwu_ref, cbg_ref, cbu_ref, o_ref, wgb_scr, wub_scr,
                   *, tiles_per_seq):
    i = pl.program_id(1)
    first = (i % tiles_per_seq) == 0

    @pl.when(i == 0)
    def _():
        wgb_scr[...] = wg_ref[...].astype(BF16)
        wub_scr[...] = wu_ref[...].astype(BF16)

    halo = jnp.where(first, jnp.zeros_like(h_ref[...]), h_ref[...])
    x = jnp.concatenate([halo, x_ref[...]], axis=0)

    def branch(w_scr, cw_ref, cb_ref):
        u = jnp.dot(x, w_scr[...], preferred_element_type=F32)
        cw = cw_ref[...]
        return pltpu.roll(u, 2, 0) * cw[0:1, :] + pltpu.roll(u, 1, 0) * cw[1:2, :] + u * cw[2:3, :] + cb_ref[...]

    gate = branch(wgb_scr, cwg_ref, cbg_ref)
    act = gate * jax.nn.sigmoid(gate) * branch(wub_scr, cwu_ref, cbu_ref)
    o_ref[...] = act[FFN_HALO:].astype(o_ref.dtype)


def ffn_up(h, w_up, conv_w, conv_b, layer, seq, tm=1024, tn=512):
    m, d = h.shape
    f = w_up.shape[-1] // 2
    nf = f // tn
    assert f % tn == 0 and seq % tm == 0
    hb = tm // FFN_HALO
    return pl.pallas_call(
        functools.partial(_ffn_up_kernel, tiles_per_seq=seq // tm),
        grid=(nf, m // tm),
        in_specs=[
            pl.BlockSpec((tm, d), lambda c, i: (i, 0)),
            pl.BlockSpec((FFN_HALO, d), lambda c, i: (jnp.maximum(i * hb - 1, 0), 0)),
            pl.BlockSpec((None, d, tn), lambda c, i: (layer, 0, c)),
            pl.BlockSpec((None, d, tn), lambda c, i: (layer, 0, c + nf)),
            pl.BlockSpec((None, CONV_WIDTH, tn), lambda c, i: (layer, 0, c)),
            pl.BlockSpec((None, CONV_WIDTH, tn), lambda c, i: (layer, 0, c + nf)),
            pl.BlockSpec((None, 1, tn), lambda c, i: (layer, 0, c)),
            pl.BlockSpec((None, 1, tn), lambda c, i: (layer, 0, c + nf)),
        ],
        out_specs=pl.BlockSpec((tm, tn), lambda c, i: (i, c)),
        out_shape=jax.ShapeDtypeStruct((m, f), BF16),
        scratch_shapes=[pltpu.VMEM((d, tn), BF16), pltpu.VMEM((d, tn), BF16)],
        compiler_params=_params("parallel", "arbitrary"),
        name="ffn_up_conv",
    )(h, h, w_up, w_up, conv_w, conv_w, conv_b[:, None, :], conv_b[:, None, :])


def conv_ffn(x, h, w_up, conv_w, conv_b, w_down, layer, seq):
    act = ffn_up(h, w_up, conv_w, conv_b, layer, seq)
    return matmul(act, w_down, layer=layer, residual=x, tm=512, tn=512)


def _rope_tables(seq):
    half = ROT_DIM // 2
    inv_freq = jnp.float32(ROPE_THETA) ** (-jnp.arange(0, ROT_DIM, 2, dtype=F32) / ROT_DIM)
    ang = jnp.arange(seq, dtype=F32)[:, None] * inv_freq[None, :]
    cos, sin = jnp.cos(ang), jnp.sin(ang)
    ones = jnp.ones((seq, MOBA_HEAD_DIM - ROT_DIM), F32)
    zeros_h = jnp.zeros((seq, half), F32)
    zeros_r = jnp.zeros((seq, MOBA_HEAD_DIM - ROT_DIM), F32)
    c_tab = jnp.concatenate([cos, cos, ones], axis=1)
    s_lo = jnp.concatenate([-sin, zeros_h, zeros_r], axis=1)
    s_hi = jnp.concatenate([zeros_h, sin, zeros_r], axis=1)
    return c_tab, s_lo, s_hi


def _moba_prep_kernel(qkv_ref, c_ref, slo_ref, shi_ref, qg_ref, kg_ref,
                      q_ref, k_ref, vt_ref, sel_ref, kmean_scr, *, blocks_per_seq):
    n = pl.program_id(0) % blocks_per_seq
    d = MOBA_HEADS * MOBA_HEAD_DIM
    half = ROT_DIM // 2
    c_tab, s_lo, s_hi = c_ref[...], slo_ref[...], shi_ref[...]

    @pl.when(n == 0)
    def _():
        kmean_scr[...] = jnp.zeros_like(kmean_scr)

    def norm_rope(x, gain):
        y = x * lax.rsqrt(jnp.mean(x * x, axis=-1, keepdims=True) + NORM_EPS) * gain
        return (y * c_tab + pltpu.roll(y, MOBA_HEAD_DIM - half, 1) * s_lo + pltpu.roll(y, half, 1) * s_hi)

    nb = sel_ref.shape[1]
    blk = lax.broadcasted_iota(jnp.int32, (nb, MOBA_BLOCK), 0)
    eligible = blk < n
    for h in range(MOBA_HEADS):
        lo = h * MOBA_HEAD_DIM
        q = norm_rope(qkv_ref[:, lo:lo + MOBA_HEAD_DIM], qg_ref[...])
        k = norm_rope(qkv_ref[:, d + lo:d + lo + MOBA_HEAD_DIM], kg_ref[...])
        q_ref[:, lo:lo + MOBA_HEAD_DIM] = (q * MOBA_Q_SCALE).astype(q_ref.dtype)
        k_ref[:, lo:lo + MOBA_HEAD_DIM] = k.astype(k_ref.dtype)
        vlo = h * MOBA_VROWS
        vt_ref[vlo:vlo + MOBA_HEAD_DIM, :] = qkv_ref[:, 2 * d + lo:2 * d + lo + MOBA_HEAD_DIM].T.astype(vt_ref.dtype)
        vt_ref[vlo + MOBA_HEAD_DIM:vlo + MOBA_VROWS, :] = jnp.ones((MOBA_VROWS - MOBA_HEAD_DIM, MOBA_BLOCK), vt_ref.dtype)
        gate = lax.dot_general(kmean_scr[:, lo:lo + MOBA_HEAD_DIM], q, (((1,), (1,)), ((), ())),
                               precision=HI, preferred_element_type=F32)
        gate = jnp.where(eligible, gate, -jnp.inf)
        sel = jnp.zeros((nb, MOBA_BLOCK), F32)
        for j in range(nb):
            gj = gate[j:j + 1, :]
            beats = jnp.where(gate > gj, 1.0, jnp.where((gate == gj) & (blk < j), 1.0, 0.0))
            rank = jnp.sum(beats, axis=0, keepdims=True)
            chosen = jnp.where((rank < MOBA_TOPK) & (j < n), 1.0, 0.0)
            sel = jnp.where(blk == j, chosen, sel)
        sel_ref[h] = (sel - 1.0) * -NEG_BIG
        km_rows = lax.broadcasted_iota(jnp.int32, (nb, MOBA_HEAD_DIM), 0)
        kmean_scr[:, lo:lo + MOBA_HEAD_DIM] = jnp.where(km_rows == n, jnp.mean(k, axis=0, keepdims=True),
                                                        kmean_scr[:, lo:lo + MOBA_HEAD_DIM])


def _moba_attn_kernel(q_ref, k_ref, vt_ref, bias_ref, o_ref, s_scr):
    i = pl.program_id(2)
    heads = range(MOBA_HEADS_PER_CALL)
    lanes = [slice(h * MOBA_HEAD_DIM, (h + 1) * MOBA_HEAD_DIM) for h in heads]
    vrows = [slice(h * MOBA_VROWS, (h + 1) * MOBA_VROWS) for h in heads]
    q = [q_ref[:, ln] for ln in lanes]
    nt_dims = (((1,), (1,)), ((), ()))
    pair = 2 * MOBA_BLOCK
    n_pairs = (i + 2) // 2
    kpos = lax.broadcasted_iota(jnp.int32, (MOBA_BLOCK, MOBA_BLOCK), 0)
    qpos = lax.broadcasted_iota(jnp.int32, (MOBA_BLOCK, MOBA_BLOCK), 1)
    causal_bias = jnp.where(kpos <= qpos, 0.0, NEG_BIG)

    def store_scores(p, m_run):
        start = pl.multiple_of(p * pair, pair)
        m_new = []
        for h in heads:
            s = lax.dot_general(k_ref[pl.ds(start, pair), lanes[h]], q[h], nt_dims, preferred_element_type=F32)
            bias = jnp.concatenate([jnp.where(2 * p + e == i, causal_bias, bias_ref[h, pl.ds(2 * p + e, 1), :])
                                    for e in range(2)], axis=0)
            s = s + bias
            s_scr[h, pl.ds(start, pair), :] = s
            m_new.append(jnp.maximum(m_run[h], jnp.max(s.reshape(pair // SUBLANES, SUBLANES, MOBA_BLOCK), axis=0)))
        return tuple(m_new)

    m_part = lax.fori_loop(0, n_pairs, store_scores,
                           tuple(jnp.full((SUBLANES, MOBA_BLOCK), NEG_BIG, F32) for _ in heads))
    m_fin = [jnp.max(x, axis=0, keepdims=True) for x in m_part]

    def accumulate(p, acc):
        start = pl.multiple_of(p * pair, pair)
        return tuple(
            acc[h] + jnp.dot(vt_ref[vrows[h], pl.ds(start, pair)],
                             jnp.exp2(s_scr[h, pl.ds(start, pair), :] - m_fin[h]).astype(BF16),
                             preferred_element_type=F32)
            for h in heads)

    acc = lax.fori_loop(0, n_pairs, accumulate,
                        tuple(jnp.zeros((MOBA_VROWS, MOBA_BLOCK), F32) for _ in heads))
    for h in heads:
        out_t = acc[h][:MOBA_HEAD_DIM] / acc[h][MOBA_HEAD_DIM:MOBA_HEAD_DIM + 1]
        o_ref[:, lanes[h]] = out_t.T.astype(o_ref.dtype)


def moba_layer(x, norm_g, wqkv, q_gain, k_gain, wo, layer, next_norm, batch, seq):
    m, d = x.shape
    nb = seq // MOBA_BLOCK
    h = rmsnorm(x, norm_g)
    qkv = matmul(h, wqkv, layer=layer)
    c_tab, s_lo, s_hi = _rope_tables(seq)
    tab_spec = pl.BlockSpec((MOBA_BLOCK, MOBA_HEAD_DIM), lambda i: (i % nb, 0))
    gain_spec = pl.BlockSpec((1, MOBA_HEAD_DIM), lambda i: (0, 0))
    q, k, vt, sel = pl.pallas_call(
        functools.partial(_moba_prep_kernel, blocks_per_seq=nb),
        grid=(m // MOBA_BLOCK,),
        in_specs=[pl.BlockSpec((MOBA_BLOCK, 3 * d), lambda i: (i, 0)), tab_spec, tab_spec, tab_spec,
                  gain_spec, gain_spec],
        out_specs=[pl.BlockSpec((MOBA_BLOCK, d), lambda i: (i, 0)),
                   pl.BlockSpec((MOBA_BLOCK, d), lambda i: (i, 0)),
                   pl.BlockSpec((MOBA_HEADS * MOBA_VROWS, MOBA_BLOCK), lambda i: (0, i)),
                   pl.BlockSpec((None, MOBA_HEADS, nb, MOBA_BLOCK), lambda i: (i, 0, 0, 0))],
        out_shape=[jax.ShapeDtypeStruct((m, d), BF16), jax.ShapeDtypeStruct((m, d), BF16),
                   jax.ShapeDtypeStruct((MOBA_HEADS * MOBA_VROWS, m), BF16),
                   jax.ShapeDtypeStruct((m // MOBA_BLOCK, MOBA_HEADS, nb, MOBA_BLOCK), F32)],
        scratch_shapes=[pltpu.VMEM((nb, d), F32)],
        compiler_params=_params("arbitrary"),
        name="moba_prep",
    )(qkv, c_tab, s_lo, s_hi, q_gain.reshape(1, -1), k_gain.reshape(1, -1))
    hw = MOBA_HEADS_PER_CALL * MOBA_HEAD_DIM
    attn = pl.pallas_call(
        _moba_attn_kernel,
        grid=(batch, MOBA_HEADS // MOBA_HEADS_PER_CALL, nb),
        in_specs=[pl.BlockSpec((MOBA_BLOCK, hw), lambda b, hh, i: (b * nb + i, hh)),
                  pl.BlockSpec((seq, hw), lambda b, hh, i: (b, hh)),
                  pl.BlockSpec((MOBA_HEADS_PER_CALL * MOBA_VROWS, seq), lambda b, hh, i: (hh, b)),
                  pl.BlockSpec((None, MOBA_HEADS_PER_CALL, nb, MOBA_BLOCK), lambda b, hh, i: (b * nb + i, hh, 0, 0))],
        out_specs=pl.BlockSpec((MOBA_BLOCK, hw), lambda b, hh, i: (b * nb + i, hh)),
        out_shape=jax.ShapeDtypeStruct((m, d), BF16),
        scratch_shapes=[pltpu.VMEM((MOBA_HEADS_PER_CALL, seq, MOBA_BLOCK), F32)],
        compiler_params=_params("parallel", "parallel", "parallel"),
        name="moba_attn",
    )(q, k, vt, sel)
    return matmul(attn, wo, layer=layer, residual=x, norm_gain=next_norm, tm=512)


def _head_sum_matrix():
    r = lax.broadcasted_iota(jnp.int32, (LANES, LANES), 0) // RWKV_HEAD_SIZE
    c = lax.broadcasted_iota(jnp.int32, (LANES, LANES), 1) // RWKV_HEAD_SIZE
    return jnp.where(r == c, 1.0, 0.0).astype(BF16)


def _split_dot(x, y_exact, pieces):
    out = None
    for _ in range(pieces):
        part = x.astype(BF16)
        term = jnp.dot(part, y_exact, preferred_element_type=F32)
        out = term if out is None else out + term
        x = x - part.astype(F32)
    return out


def _head_sum(x, ones_bd):
    return _split_dot(x, ones_bd, 2)


def _rwkv_mix_kernel(x_ref, h_ref, g_ref, mu_ref, *o_refs, tiles_per_seq):
    first = (pl.program_id(0) % tiles_per_seq) == 0

    def norm(x):
        return (x * lax.rsqrt(jnp.mean(x * x, axis=-1, keepdims=True) + NORM_EPS) * g_ref[...])

    h = norm(x_ref[...])
    prev_row = jnp.where(first, 0.0, norm(h_ref[...])[SUBLANES - 1:SUBLANES, :])
    row = lax.broadcasted_iota(jnp.int32, h.shape, 0)
    xx = jnp.where(row == 0, prev_row, pltpu.roll(h, 1, 0)) - h
    for idx, o_ref in enumerate(o_refs):
        o_ref[...] = (h + xx * mu_ref[idx:idx + 1, :]).astype(o_ref.dtype)


def _rwkv_prep_kernel(k_ref, wp_ref, ap_ref, w0_ref, a0_ref, kk_ref, ka_ref,
                      logw_ref, kmod_ref, an_ref, b_ref):
    ones_bd = _head_sum_matrix()
    w_log = -jax.nn.softplus(-(w0_ref[...] + wp_ref[...])) - 0.5
    logw_ref[...] = -jnp.exp(w_log)
    a = jax.nn.sigmoid(a0_ref[...] + ap_ref[...])
    k = k_ref[...]
    kmod_ref[...] = k * (1.0 + (a - 1.0) * ka_ref[...])
    kk = k * kk_ref[...]
    for c in range(kk.shape[1] // LANES):
        sl = slice(c * LANES, (c + 1) * LANES)
        kc = kk[:, sl]
        nrm = jnp.maximum(jnp.sqrt(_head_sum(kc * kc, ones_bd)), 1e-12)
        kc = kc / nrm
        an_ref[:, sl] = -kc
        b_ref[:, sl] = kc * a[:, sl]


def _rwkv_chunk_kernel(r_ref, lw_ref, k_ref, v_ref, a_ref, b_ref, y_ref, s_scr):
    L, n, grp = RWKV_CHUNK, RWKV_HEAD_SIZE, RWKV_GROUP
    width = grp * n
    rows = grp * L
    tb = r_ref.shape[0]

    @pl.when(pl.program_id(2) == 0)
    def _():
        s_scr[...] = jnp.zeros_like(s_scr)

    row = lax.broadcasted_iota(jnp.int32, (rows, rows), 0)
    col = lax.broadcasted_iota(jnp.int32, (rows, rows), 1)
    strict = (col % L) < (row % L)
    incl = (col % L) <= (row % L)
    lane_head = lax.broadcasted_iota(jnp.int32, (L, width), 1) // n
    cum = jnp.where(lax.broadcasted_iota(jnp.int32, (L, L), 1) <= lax.broadcasted_iota(jnp.int32, (L, L), 0),
                    1.0, 0.0).astype(BF16)

    def cumsum_rows(x):
        out = None
        for _ in range(3):
            part = x.astype(BF16)
            term = jnp.dot(cum, part, preferred_element_type=F32)
            out = term if out is None else out + term
            x = x - part.astype(F32)
        return out
    nt_dims = (((1,), (1,)), ((), ()))
    tn_dims = (((0,), (0,)), ((), ()))

    def stack(x):
        return jnp.concatenate([jnp.where(lane_head == h, x, 0.0) for h in range(grp)], axis=0)

    def unstack(xm):
        out = xm[0:L]
        for h in range(1, grp):
            out = out + xm[h * L:(h + 1) * L]
        return out

    def mm(x, y):
        return jnp.dot(x.astype(BF16), y.astype(BF16), preferred_element_type=F32)

    n_groups = r_ref.shape[1] // width
    groups = range(n_groups)

    def chunk(c, carry):
        sl = pl.ds(pl.multiple_of(c * L, L), L)
        lanes = [slice(q * width, (q + 1) * width) for q in groups]
        lw = [lw_ref[sl, ln] for ln in lanes]
        g = [cumsum_rows(x) for x in lw]
        g_last = [x[L - 1:L, :] for x in g]
        e_neg = [jnp.exp(-x) for x in g]
        r, k, v, a, b = ([ref[sl, ln] for ln in lanes] for ref in (r_ref, k_ref, v_ref, a_ref, b_ref))
        lhs = [jnp.concatenate([stack(a[q] * jnp.exp(g[q] - lw[q])), stack(r[q] * jnp.exp(g[q]))], axis=0).astype(BF16)
               for q in groups]
        rhs = [jnp.concatenate([stack(b[q] * e_neg[q]), stack(k[q] * e_neg[q])], axis=0).astype(BF16) for q in groups]
        v_m = [stack(x) for x in v]
        s = [s_scr[q] for q in groups]
        prod = [lax.dot_general(lhs[q], rhs[q], nt_dims, preferred_element_type=F32) for q in groups]
        from_state = [lax.dot_general(lhs[q], s[q].astype(BF16), nt_dims, preferred_element_type=F32) for q in groups]
        a_ak = [jnp.where(strict, p[:rows, rows:], 0.0) for p in prod]
        u_m = [from_state[q][:rows] + mm(a_ak[q], v_m[q]) for q in groups]
        power = [jnp.where(strict, p[:rows, :rows], 0.0) for p in prod]
        for level in range(L.bit_length() - 1):
            if level:
                power = [mm(p, p) for p in power]
            u_m = [u_m[q] + mm(power[q], u_m[q]) for q in groups]
        for q in groups:
            m_both = jnp.where(jnp.concatenate([incl, incl], axis=1), prod[q][rows:, :], 0.0).astype(BF16)
            uv = jnp.concatenate([u_m[q], v_m[q]], axis=0).astype(BF16)
            y_ref[sl, lanes[q]] = unstack(from_state[q][rows:] + jnp.dot(m_both, uv, preferred_element_type=F32))
            e_tail = jnp.exp(g_last[q] - g[q])
            tail = jnp.concatenate([stack(b[q] * e_tail), stack(k[q] * e_tail)], axis=0).astype(BF16)
            s_scr[q] = s[q] * jnp.exp(g_last[q]) + lax.dot_general(uv, tail, tn_dims, preferred_element_type=F32)
        return carry

    lax.fori_loop(0, tb // L, chunk, 0)


def _rwkv_post_kernel(y_ref, r_ref, kmod_ref, v_ref, g_ref, rk_ref, lw_ref, lb_ref, o_ref):
    ones_bd = _head_sum_matrix()
    inv_n = 1.0 / RWKV_HEAD_SIZE
    for c in range(y_ref.shape[1] // LANES):
        sl = slice(c * LANES, (c + 1) * LANES)
        y = y_ref[:, sl]
        mean = _head_sum(y, ones_bd) * inv_n
        yc = y - mean
        var = _head_sum(yc * yc, ones_bd) * inv_n
        yn = yc * lax.rsqrt(var + RWKV_GN_EPS) * lw_ref[:, sl] + lb_ref[:, sl]
        bonus = _head_sum(r_ref[:, sl] * kmod_ref[:, sl] * rk_ref[:, sl], ones_bd) * v_ref[:, sl]
        o_ref[:, sl] = ((yn + bonus) * g_ref[:, sl]).astype(o_ref.dtype)


def rwkv_layer(x, norm_g, mu, w_r, w_k, w_v, w_o, w0, w1, w2, a0, a1, a2, g1, g2,
               k_k, k_a, r_k, lnx_w, lnx_b, next_norm, batch, seq):
    m, d = x.shape
    tm = 256
    row_spec = pl.BlockSpec((tm, d), lambda i: (i, 0))
    vec_spec = pl.BlockSpec((1, d), lambda i: (0, 0))
    hb = tm // SUBLANES
    mixed = pl.pallas_call(
        functools.partial(_rwkv_mix_kernel, tiles_per_seq=seq // tm),
        grid=(m // tm,),
        in_specs=[row_spec, pl.BlockSpec((SUBLANES, d), lambda i: (jnp.maximum(i * hb - 1, 0), 0)),
                  vec_spec, pl.BlockSpec((6, d), lambda i: (0, 0))],
        out_specs=[row_spec] * 6,
        out_shape=[jax.ShapeDtypeStruct((m, d), BF16)] * 6,
        compiler_params=_params("parallel"),
        name="rwkv_mix",
    )(x, x, norm_g.reshape(1, d), mu)
    x_r, x_w, x_k, x_v, x_a, x_g = mixed

    def lora_pad(w_in, w_out):
        rank = -(-w_in.shape[1] // LANES) * LANES
        return _pad_cols(w_in, rank), _pad_rows(w_out, rank)

    r = matmul(x_r, w_r)
    k = matmul(x_k, w_k)
    v = matmul(x_v, w_v)
    w1p, w2p = lora_pad(w1, w2)
    a1p, a2p = lora_pad(a1, a2)
    g1p, g2p = lora_pad(g1, g2)
    w_pre = matmul(matmul(x_w, w1p, act="tanh", out_dtype=BF16), w2p)
    a_pre = matmul(matmul(x_a, a1p, out_dtype=BF16), a2p)
    gate = matmul(matmul(x_g, g1p, act="sigmoid", out_dtype=BF16), g2p)

    log_w, k_mod, a_neg, b_vec = pl.pallas_call(
        _rwkv_prep_kernel,
        grid=(m // tm,),
        in_specs=[row_spec] * 3 + [vec_spec] * 4,
        out_specs=[row_spec] * 4,
        out_shape=[jax.ShapeDtypeStruct((m, d), F32)] * 4,
        compiler_params=_params("parallel"),
        name="rwkv_prep",
    )(k, w_pre, a_pre, w0.reshape(1, d), a0.reshape(1, d), k_k.reshape(1, d), k_a.reshape(1, d))

    tb, groups_per_call = 256, 4
    width = groups_per_call * RWKV_GROUP * RWKV_HEAD_SIZE
    nt = seq // tb
    blk = pl.BlockSpec((tb, width), lambda b, p, t: (b * nt + t, p))
    y = pl.pallas_call(
        _rwkv_chunk_kernel,
        grid=(batch, d // width, nt),
        in_specs=[blk] * 6,
        out_specs=blk,
        out_shape=jax.ShapeDtypeStruct((m, d), F32),
        scratch_shapes=[pltpu.VMEM((groups_per_call, RWKV_GROUP * RWKV_HEAD_SIZE, RWKV_GROUP * RWKV_HEAD_SIZE), F32)],
        compiler_params=_params("parallel", "parallel", "arbitrary"),
        name="rwkv_chunks",
    )(r, log_w, k_mod, v, a_neg, b_vec)

    out = pl.pallas_call(
        _rwkv_post_kernel,
        grid=(m // tm,),
        in_specs=[row_spec] * 5 + [vec_spec] * 3,
        out_specs=row_spec,
        out_shape=jax.ShapeDtypeStruct((m, d), BF16),
        compiler_params=_params("parallel"),
        name="rwkv_post",
    )(y, r, k_mod, v, gate, r_k.reshape(1, d), lnx_w.reshape(1, d), lnx_b.reshape(1, d))
    return matmul(out, w_o, residual=x, norm_gain=next_norm, tm=512)


def _softcap(z):
    return GATE_SOFTCAP * jnp.tanh(z / GATE_SOFTCAP)


def _mlstm_kernel(q_ref, k_ref, v_ref, o_ref, gc_ref, gr_ref, hg_ref, out_ref, ct_scr, n_scr, m_scr):
    L, H, dk, dv = MLSTM_CHUNK, MLSTM_HEADS, MLSTM_DQK, MLSTM_DV
    heads = range(H)

    @pl.when(pl.program_id(1) == 0)
    def _():
        ct_scr[...] = jnp.zeros_like(ct_scr)
        n_scr[...] = jnp.zeros_like(n_scr)
        m_scr[...] = jnp.zeros_like(m_scr)

    t_idx = lax.broadcasted_iota(jnp.int32, (L, L), 0)
    s_idx = lax.broadcasted_iota(jnp.int32, (L, L), 1)
    causal = s_idx <= t_idx
    anti = t_idx <= s_idx
    nt_dims = (((1,), (1,)), ((), ()))

    def chunk(c, carry):
        sl = pl.ds(pl.multiple_of(c * L, L), L)
        gc, gr = gc_ref[sl, :], gr_ref[c]
        li_cols, li_rows = _softcap(gc[:, :H]), _softcap(gr[:H, :])
        lf_cols = jax.nn.log_sigmoid(_softcap(gc[:, H:]))
        lf_rows = jax.nn.log_sigmoid(_softcap(gr[H:, :]))
        li_col = [li_cols[:, h:h + 1] for h in heads]
        li_row = [li_rows[h:h + 1, :] for h in heads]
        lf_col = [lf_cols[:, h:h + 1] for h in heads]
        lf_row = [lf_rows[h:h + 1, :] for h in heads]
        b_col = [jnp.sum(jnp.where(causal, lf_row[h], 0.0), axis=1, keepdims=True) for h in heads]
        b_row = [jnp.sum(jnp.where(anti, lf_col[h], 0.0), axis=0, keepdims=True) for h in heads]
        b_last = [jnp.sum(lf_row[h], axis=1, keepdims=True) for h in heads]
        m_prev = [m_scr[h] for h in heads]
        dmat = [jnp.where(causal, b_col[h] - b_row[h] + li_row[h], NEG_BIG) for h in heads]
        inter = [b_col[h] + m_prev[h] for h in heads]
        m_t = [jnp.maximum(inter[h], jnp.max(dmat[h], axis=1, keepdims=True)) for h in heads]
        q = [q_ref[sl, h * dk:(h + 1) * dk] for h in heads]
        k = [k_ref[sl, h * dk:(h + 1) * dk] * (dk ** -0.5) for h in heads]
        vb = [v_ref[sl, h * dv:(h + 1) * dv].astype(BF16) for h in heads]
        qb = [x.astype(BF16) for x in q]
        s = [lax.dot_general(qb[h], k[h].astype(BF16), nt_dims, preferred_element_type=F32) * jnp.exp(dmat[h] - m_t[h])
             for h in heads]
        w_inter = [jnp.exp(inter[h] - m_t[h]) for h in heads]
        ct = [ct_scr[h] for h in heads]
        n_row = [n_scr[h] for h in heads]
        num = [jnp.dot(s[h].astype(BF16), vb[h], preferred_element_type=F32)
               + w_inter[h] * jnp.dot(qb[h], ct[h].astype(BF16), preferred_element_type=F32) for h in heads]
        den = [jnp.sum(s[h], axis=1, keepdims=True) + w_inter[h] * jnp.sum(q[h] * n_row[h], axis=1, keepdims=True)
               for h in heads]
        h_c = [num[h] / jnp.maximum(jnp.abs(den[h]), jnp.exp(-m_t[h])) for h in heads]
        for h in heads:
            hn = h_c[h] * lax.rsqrt(jnp.mean(h_c[h] * h_c[h], axis=-1, keepdims=True) + NORM_EPS)
            cols = slice(h * dv, (h + 1) * dv)
            out_ref[sl, cols] = (hn * hg_ref[:, cols] * jax.nn.sigmoid(o_ref[sl, cols])).astype(out_ref.dtype)
        d_row = [b_last[h] - b_row[h] + li_row[h] for h in heads]
        m_new = [jnp.maximum(b_last[h] + m_prev[h], jnp.max(d_row[h], axis=1, keepdims=True)) for h in heads]
        kw = [k[h] * jnp.exp(b_last[h] - b_col[h] + li_col[h] - m_new[h]) for h in heads]
        w_c = [jnp.exp(b_last[h] + m_prev[h] - m_new[h]) for h in heads]
        for h in heads:
            ct_scr[h] = w_c[h] * ct[h] + jnp.dot(kw[h].T.astype(BF16), vb[h], preferred_element_type=F32)
            n_scr[h] = w_c[h] * n_row[h] + jnp.sum(kw[h], axis=0, keepdims=True)
            m_scr[h] = m_new[h]
        return carry

    lax.fori_loop(0, q_ref.shape[0] // L, chunk, 0)


def mlstm_layer(x, norm_g, w_in, b_if, head_gain, w_out, next_norm, batch, seq):
    m, d = x.shape
    H, L, dk, dv = MLSTM_HEADS, MLSTM_CHUNK, MLSTM_DQK, MLSTM_DV
    tb = 4 * L
    nt = seq // tb
    h = rmsnorm(x, norm_g)
    n_main = 2 * H * dk + 2 * H * dv
    proj = matmul(h, w_in, n=n_main)
    gates = matmul(h, _pad_cols(w_in[:, n_main:], LANES), bias=_pad_cols(b_if.reshape(1, 2 * H), LANES))
    gates = gates[:, :2 * H]
    gates_t = jnp.transpose(gates.reshape(m // L, L, 2 * H), (0, 2, 1))
    out = pl.pallas_call(
        _mlstm_kernel,
        grid=(batch, nt),
        in_specs=[pl.BlockSpec((tb, H * dk), lambda b, t: (b * nt + t, 0)),
                  pl.BlockSpec((tb, H * dk), lambda b, t: (b * nt + t, 1)),
                  pl.BlockSpec((tb, H * dv), lambda b, t: (b * nt + t, 1)),
                  pl.BlockSpec((tb, H * dv), lambda b, t: (b * nt + t, 2)),
                  pl.BlockSpec((tb, 2 * H), lambda b, t: (b * nt + t, 0)),
                  pl.BlockSpec((tb // L, 2 * H, L), lambda b, t: (b * nt + t, 0, 0)),
                  pl.BlockSpec((1, H * dv), lambda b, t: (0, 0))],
        out_specs=pl.BlockSpec((tb, H * dv), lambda b, t: (b * nt + t, 0)),
        out_shape=jax.ShapeDtypeStruct((m, H * dv), BF16),
        scratch_shapes=[pltpu.VMEM((H, dk, dv), F32), pltpu.VMEM((H, 1, dk), F32), pltpu.VMEM((H, 1, 1), F32)],
        compiler_params=_params("parallel", "arbitrary"),
        name="mlstm_chunks",
    )(proj, proj, proj, proj, gates, gates_t, head_gain.reshape(1, -1))
    return matmul(out, w_out, residual=x, norm_gain=next_norm, tm=512)


def kernel(x, moba_norm, moba_wqkv, moba_q_gain, moba_k_gain, moba_wo, rwkv_norm, rwkv_mu, rwkv_w_r, rwkv_w_k, rwkv_w_v, rwkv_w_o, rwkv_w0, rwkv_w1, rwkv_w2, rwkv_a0, rwkv_a1, rwkv_a2, rwkv_g1, rwkv_g2, rwkv_k_k, rwkv_k_a, rwkv_r_k, rwkv_lnx_w, rwkv_lnx_b, mlstm_norm, mlstm_w_in, mlstm_b_if, mlstm_head_gain, mlstm_w_out, ffn_norm, ffn_w_up, ffn_conv_w, ffn_conv_b, ffn_w_down):
    batch, seq, d = x.shape
    depth = ffn_norm.shape[0]
    x = x.reshape(batch * seq, d)
    for i in range(depth):
        kind, j = i % 3, i // 3
        if kind == 0:
            x, h = moba_layer(x, moba_norm[j], moba_wqkv, moba_q_gain[j], moba_k_gain[j], moba_wo, j,
                              ffn_norm[i], batch, seq)
        elif kind == 1:
            x, h = rwkv_layer(x, rwkv_norm[j], rwkv_mu[j], rwkv_w_r[j], rwkv_w_k[j], rwkv_w_v[j], rwkv_w_o[j],
                           rwkv_w0[j], rwkv_w1[j], rwkv_w2[j], rwkv_a0[j], rwkv_a1[j], rwkv_a2[j],
                           rwkv_g1[j], rwkv_g2[j], rwkv_k_k[j], rwkv_k_a[j], rwkv_r_k[j],
                           rwkv_lnx_w[j], rwkv_lnx_b[j], ffn_norm[i], batch, seq)
        else:
            x, h = mlstm_layer(x, mlstm_norm[j], mlstm_w_in[j], mlstm_b_if[j], mlstm_head_gain[j],
                               mlstm_w_out[j], ffn_norm[i], batch, seq)
        x = conv_ffn(x, h, ffn_w_up, ffn_conv_w, ffn_conv_b, ffn_w_down, i, seq)
    return x.reshape(batch, seq, d)
```

```python
import functools

import jax
import jax.numpy as jnp
from jax import lax
from jax.experimental import pallas as pl
from jax.experimental.pallas import tpu as pltpu

F32 = jnp.float32
BF16 = jnp.bfloat16

LANES = 128
SUBLANES = 8
VMEM_LIMIT_BYTES = 56 * 1024 * 1024

NORM_EPS = 1e-6
MOBA_HEADS = 16
MOBA_HEAD_DIM = 128
MOBA_BLOCK = 256
MOBA_TOPK = 3
MOBA_HEADS_PER_CALL = 4
MOBA_VROWS = MOBA_HEAD_DIM + 16
MOBA_Q_SCALE = 1.4426950408889634 * MOBA_HEAD_DIM ** -0.5
ROPE_THETA = 500000.0
ROT_DIM = MOBA_HEAD_DIM // 4
RWKV_HEAD_SIZE = 64
RWKV_GN_EPS = 64e-5
RWKV_CHUNK = 64
RWKV_GROUP = 4
MLSTM_HEADS = 8
MLSTM_DV = 256
MLSTM_DQK = 128
MLSTM_CHUNK = 64
GATE_SOFTCAP = 15.0
CONV_WIDTH = 3
FFN_HALO = 16
NEG_BIG = -1e30

HI = lax.Precision.HIGHEST


def _params(*semantics):
    return pltpu.CompilerParams(dimension_semantics=semantics, vmem_limit_bytes=VMEM_LIMIT_BYTES)


def _pad_cols(w, n):
    return jnp.pad(w, ((0, 0), (0, n - w.shape[1])))


def _pad_rows(w, n):
    return jnp.pad(w, ((0, n - w.shape[0]), (0, 0)))


def _rmsnorm_kernel(x_ref, g_ref, o_ref):
    x = x_ref[...]
    y = x * lax.rsqrt(jnp.mean(x * x, axis=-1, keepdims=True) + NORM_EPS)
    o_ref[...] = (y * g_ref[...]).astype(o_ref.dtype)


def rmsnorm(x, gain, tm=512):
    m, d = x.shape
    return pl.pallas_call(
        _rmsnorm_kernel,
        grid=(m // tm,),
        in_specs=[pl.BlockSpec((tm, d), lambda i: (i, 0)), pl.BlockSpec((1, d), lambda i: (0, 0))],
        out_specs=pl.BlockSpec((tm, d), lambda i: (i, 0)),
        out_shape=jax.ShapeDtypeStruct((m, d), BF16),
        compiler_params=_params("parallel"),
        name="rmsnorm",
    )(x, gain.reshape(1, d))


def _mm_kernel(*refs, act, has_bias, has_res, has_norm, cast_w):
    x_ref, w_ref = refs[0], refs[1]
    if cast_w:
        w_ref, refs = refs[-1], refs[:-1]

        @pl.when(pl.program_id(1) == 0)
        def _():
            w_ref[...] = refs[1][...].astype(BF16)

    acc = jnp.dot(x_ref[...], w_ref[...], preferred_element_type=F32)
    k = 2
    if has_bias:
        acc = acc + refs[k][...]
        k += 1
    if act == "tanh":
        acc = jnp.tanh(acc)
    elif act == "sigmoid":
        acc = jax.nn.sigmoid(acc)
    if has_res:
        acc = acc + refs[k][...]
        k += 1
    if has_norm:
        o_ref, n_ref = refs[-2], refs[-1]
        o_ref[...] = acc.astype(o_ref.dtype)
        y = acc * lax.rsqrt(jnp.mean(acc * acc, axis=-1, keepdims=True) + NORM_EPS)
        n_ref[...] = (y * refs[k][...]).astype(n_ref.dtype)
    else:
        refs[-1][...] = acc.astype(refs[-1].dtype)


def matmul(x, w, *, layer=None, n=None, bias=None, act=None, residual=None, norm_gain=None, out_dtype=F32,
           tm=1024, tn=1024):
    m, kdim = x.shape
    n = w.shape[-1] if n is None else n
    tm, tn = min(tm, m), min(tn, n)
    if norm_gain is not None:
        tn = n
    assert m % tm == 0 and n % tn == 0, (m, n, tm, tn)
    cast_w = w.dtype != BF16
    if layer is None:
        w_spec = pl.BlockSpec((kdim, tn), lambda j, i: (0, j))
    else:
        w_spec = pl.BlockSpec((None, kdim, tn), lambda j, i: (layer, 0, j))
    in_specs = [pl.BlockSpec((tm, kdim), lambda j, i: (i, 0)), w_spec]
    args = [x, w]
    if bias is not None:
        in_specs.append(pl.BlockSpec((1, tn), lambda j, i: (0, j)))
        args.append(bias.reshape(1, n))
    if residual is not None:
        in_specs.append(pl.BlockSpec((tm, tn), lambda j, i: (i, j)))
        args.append(residual)
    out_spec = pl.BlockSpec((tm, tn), lambda j, i: (i, j))
    out_specs, out_shape = out_spec, jax.ShapeDtypeStruct((m, n), out_dtype)
    if norm_gain is not None:
        in_specs.append(pl.BlockSpec((1, tn), lambda j, i: (0, j)))
        args.append(norm_gain.reshape(1, n))
        out_specs, out_shape = [out_spec, out_spec], [out_shape, jax.ShapeDtypeStruct((m, n), BF16)]
    return pl.pallas_call(
        functools.partial(_mm_kernel, act=act, has_bias=bias is not None, has_res=residual is not None,
                          has_norm=norm_gain is not None, cast_w=cast_w),
        grid=(n // tn, m // tm),
        in_specs=in_specs,
        out_specs=out_specs,
        out_shape=out_shape,
        scratch_shapes=[pltpu.VMEM((kdim, tn), BF16)] if cast_w else [],
        compiler_params=_params("parallel", "arbitrary"),
        name="matmul",
    )(*args)


def _ffn_up_kernel(x_ref, h_ref, wg_ref, wu_ref, cwg_ref, cwu_ref, cbg_ref, cbu_ref, o_ref, wgb_scr, wub_scr,
                   *, tiles_per_seq):
    i = pl.program_id(1)
    first = (i % tiles_per_seq) == 0

    @pl.when(i == 0)
    def _():
        wgb_scr[...] = wg_ref[...].astype(BF16)
        wub_scr[...] = wu_ref[...].astype(BF16)

    halo = jnp.where(first, jnp.zeros_like(h_ref[...]), h_ref[...])
    x = jnp.concatenate([halo, x_ref[...]], axis=0)

    def branch(w_scr, cw_ref, cb_ref):
        u = jnp.dot(x, w_scr[...], preferred_element_type=F32)
        cw = cw_ref[...]
        return pltpu.roll(u, 2, 0) * cw[0:1, :] + pltpu.roll(u, 1, 0) * cw[1:2, :] + u * cw[2:3, :] + cb_ref[...]

    gate = branch(wgb_scr, cwg_ref, cbg_ref)
    act = gate * jax.nn.sigmoid(gate) * branch(wub_scr, cwu_ref, cbu_ref)
    o_ref[...] = act[FFN_HALO:].astype(o_ref.dtype)


def ffn_up(h, w_up, conv_w, conv_b, layer, seq, tm=1024, tn=512):
    m, d = h.shape
    f = w_up.shape[-1] // 2
    nf = f // tn
    assert f % tn == 0 and seq % tm == 0
    hb = tm // FFN_HALO
    return pl.pallas_call(
        functools.partial(_ffn_up_kernel, tiles_per_seq=seq // tm),
        grid=(nf, m // tm),
        in_specs=[
            pl.BlockSpec((tm, d), lambda c, i: (i, 0)),
            pl.BlockSpec((FFN_HALO, d), lambda c, i: (jnp.maximum(i * hb - 1, 0), 0)),
            pl.BlockSpec((None, d, tn), lambda c, i: (layer, 0, c)),
            pl.BlockSpec((None, d, tn), lambda c, i: (layer, 0, c + nf)),
            pl.BlockSpec((None, CONV_WIDTH, tn), lambda c, i: (layer, 0, c)),
            pl.BlockSpec((None, CONV_WIDTH, tn), lambda c, i: (layer, 0, c + nf)),
            pl.BlockSpec((None, 1, tn), lambda c, i: (layer, 0, c)),
            pl.BlockSpec((None, 1, tn), lambda c, i: (layer, 0, c + nf)),
        ],
        out_specs=pl.BlockSpec((tm, tn), lambda c, i: (i, c)),
        out_shape=jax.ShapeDtypeStruct((m, f), BF16),
        scratch_shapes=[pltpu.VMEM((d, tn), BF16), pltpu.VMEM((d, tn), BF16)],
        compiler_params=_params("parallel", "arbitrary"),
        name="ffn_up_conv",
    )(h, h, w_up, w_up, conv_w, conv_w, conv_b[:, None, :], conv_b[:, None, :])


def conv_ffn(x, h, w_up, conv_w, conv_b, w_down, layer, seq):
    act = ffn_up(h, w_up, conv_w, conv_b, layer, seq)
    return matmul(act, w_down, layer=layer, residual=x, tm=512, tn=512)


def _rope_tables(seq):
    half = ROT_DIM // 2
    inv_freq = jnp.float32(ROPE_THETA) ** (-jnp.arange(0, ROT_DIM, 2, dtype=F32) / ROT_DIM)
    ang = jnp.arange(seq, dtype=F32)[:, None] * inv_freq[None, :]
    cos, sin = jnp.cos(ang), jnp.sin(ang)
    ones = jnp.ones((seq, MOBA_HEAD_DIM - ROT_DIM), F32)
    zeros_h = jnp.zeros((seq, half), F32)
    zeros_r = jnp.zeros((seq, MOBA_HEAD_DIM - ROT_DIM), F32)
    c_tab = jnp.concatenate([cos, cos, ones], axis=1)
    s_lo = jnp.concatenate([-sin, zeros_h, zeros_r], axis=1)
    s_hi = jnp.concatenate([zeros_h, sin, zeros_r], axis=1)
    return c_tab, s_lo, s_hi


def _moba_prep_kernel(qkv_ref, c_ref, slo_ref, shi_ref, qg_ref, kg_ref,
                      q_ref, k_ref, vt_ref, sel_ref, kmean_scr, *, blocks_per_seq):
    n = pl.program_id(0) % blocks_per_seq
    d = MOBA_HEADS * MOBA_HEAD_DIM
    half = ROT_DIM // 2
    c_tab, s_lo, s_hi = c_ref[...], slo_ref[...], shi_ref[...]

    @pl.when(n == 0)
    def _():
        kmean_scr[...] = jnp.zeros_like(kmean_scr)

    def norm_rope(x, gain):
        y = x * lax.rsqrt(jnp.mean(x * x, axis=-1, keepdims=True) + NORM_EPS) * gain
        return (y * c_tab + pltpu.roll(y, MOBA_HEAD_DIM - half, 1) * s_lo + pltpu.roll(y, half, 1) * s_hi)

    nb = sel_ref.shape[1]
    blk = lax.broadcasted_iota(jnp.int32, (nb, MOBA_BLOCK), 0)
    eligible = blk < n
    for h in range(MOBA_HEADS):
        lo = h * MOBA_HEAD_DIM
        q = norm_rope(qkv_ref[:, lo:lo + MOBA_HEAD_DIM].astype(F32), qg_ref[...])
        k = norm_rope(qkv_ref[:, d + lo:d + lo + MOBA_HEAD_DIM].astype(F32), kg_ref[...])
        q_ref[:, lo:lo + MOBA_HEAD_DIM] = (q * MOBA_Q_SCALE).astype(q_ref.dtype)
        k_ref[:, lo:lo + MOBA_HEAD_DIM] = k.astype(k_ref.dtype)
        vlo = h * MOBA_VROWS
        vt_ref[vlo:vlo + MOBA_HEAD_DIM, :] = qkv_ref[:, 2 * d + lo:2 * d + lo + MOBA_HEAD_DIM].astype(F32).T.astype(vt_ref.dtype)
        vt_ref[vlo + MOBA_HEAD_DIM:vlo + MOBA_VROWS, :] = jnp.ones((MOBA_VROWS - MOBA_HEAD_DIM, MOBA_BLOCK), vt_ref.dtype)
        gate = lax.dot_general(kmean_scr[:, lo:lo + MOBA_HEAD_DIM], q, (((1,), (1,)), ((), ())),
                               precision=HI, preferred_element_type=F32)
        gate = jnp.where(eligible, gate, -jnp.inf)
        sel = jnp.zeros((nb, MOBA_BLOCK), F32)
        for j in range(nb):
            gj = gate[j:j + 1, :]
            beats = jnp.where(gate > gj, 1.0, jnp.where((gate == gj) & (blk < j), 1.0, 0.0))
            rank = jnp.sum(beats, axis=0, keepdims=True)
            chosen = jnp.where((rank < MOBA_TOPK) & (j < n), 1.0, 0.0)
            sel = jnp.where(blk == j, chosen, sel)
        sel_ref[h] = (sel - 1.0) * -NEG_BIG
        km_rows = lax.broadcasted_iota(jnp.int32, (nb, MOBA_HEAD_DIM), 0)
        kmean_scr[:, lo:lo + MOBA_HEAD_DIM] = jnp.where(km_rows == n, jnp.mean(k, axis=0, keepdims=True),
                                                        kmean_scr[:, lo:lo + MOBA_HEAD_DIM])


def _moba_attn_kernel(q_ref, k_ref, vt_ref, bias_ref, o_ref, s_scr):
    i = pl.program_id(2)
    heads = range(MOBA_HEADS_PER_CALL)
    lanes = [slice(h * MOBA_HEAD_DIM, (h + 1) * MOBA_HEAD_DIM) for h in heads]
    vrows = [slice(h * MOBA_VROWS, (h + 1) * MOBA_VROWS) for h in heads]
    q = [q_ref[:, ln] for ln in lanes]
    nt_dims = (((1,), (1,)), ((), ()))
    pair = 2 * MOBA_BLOCK
    n_pairs = (i + 2) // 2
    kpos = lax.broadcasted_iota(jnp.int32, (MOBA_BLOCK, MOBA_BLOCK), 0)
    qpos = lax.broadcasted_iota(jnp.int32, (MOBA_BLOCK, MOBA_BLOCK), 1)
    causal_bias = jnp.where(kpos <= qpos, 0.0, NEG_BIG)

    def store_scores(p, m_run):
        start = pl.multiple_of(p * pair, pair)
        m_new = []
        for h in heads:
            s = lax.dot_general(k_ref[pl.ds(start, pair), lanes[h]], q[h], nt_dims, preferred_element_type=F32)
            bias = jnp.concatenate([jnp.where(2 * p + e == i, causal_bias, bias_ref[h, pl.ds(2 * p + e, 1), :])
                                    for e in range(2)], axis=0)
            s = s + bias
            s_scr[h, pl.ds(start, pair), :] = s
            m_new.append(jnp.maximum(m_run[h], jnp.max(s.reshape(pair // SUBLANES, SUBLANES, MOBA_BLOCK), axis=0)))
        return tuple(m_new)

    m_part = lax.fori_loop(0, n_pairs, store_scores,
                           tuple(jnp.full((SUBLANES, MOBA_BLOCK), NEG_BIG, F32) for _ in heads))
    m_fin = [jnp.max(x, axis=0, keepdims=True) for x in m_part]

    def accumulate(p, acc):
        start = pl.multiple_of(p * pair, pair)
        return tuple(
            acc[h] + jnp.dot(vt_ref[vrows[h], pl.ds(start, pair)],
                             jnp.exp2(s_scr[h, pl.ds(start, pair), :] - m_fin[h]).astype(BF16),
                             preferred_element_type=F32)
            for h in heads)

    acc = lax.fori_loop(0, n_pairs, accumulate,
                        tuple(jnp.zeros((MOBA_VROWS, MOBA_BLOCK), F32) for _ in heads))
    for h in heads:
        out_t = acc[h][:MOBA_HEAD_DIM] / acc[h][MOBA_HEAD_DIM:MOBA_HEAD_DIM + 1]
        o_ref[:, lanes[h]] = out_t.T.astype(o_ref.dtype)


def moba_layer(x, norm_g, wqkv, q_gain, k_gain, wo, layer, next_norm, batch, seq):
    m, d = x.shape
    nb = seq // MOBA_BLOCK
    h = rmsnorm(x, norm_g)
    qkv = matmul(h, wqkv, layer=layer, out_dtype=BF16)
    c_tab, s_lo, s_hi = _rope_tables(seq)
    tab_spec = pl.BlockSpec((MOBA_BLOCK, MOBA_HEAD_DIM), lambda i: (i % nb, 0))
    gain_spec = pl.BlockSpec((1, MOBA_HEAD_DIM), lambda i: (0, 0))
    q, k, vt, sel = pl.pallas_call(
        functools.partial(_moba_prep_kernel, blocks_per_seq=nb),
        grid=(m // MOBA_BLOCK,),
        in_specs=[pl.BlockSpec((MOBA_BLOCK, 3 * d), lambda i: (i, 0)), tab_spec, tab_spec, tab_spec,
                  gain_spec, gain_spec],
        out_specs=[pl.BlockSpec((MOBA_BLOCK, d), lambda i: (i, 0)),
                   pl.BlockSpec((MOBA_BLOCK, d), lambda i: (i, 0)),
                   pl.BlockSpec((MOBA_HEADS * MOBA_VROWS, MOBA_BLOCK), lambda i: (0, i)),
                   pl.BlockSpec((None, MOBA_HEADS, nb, MOBA_BLOCK), lambda i: (i, 0, 0, 0))],
        out_shape=[jax.ShapeDtypeStruct((m, d), BF16), jax.ShapeDtypeStruct((m, d), BF16),
                   jax.ShapeDtypeStruct((MOBA_HEADS * MOBA_VROWS, m), BF16),
                   jax.ShapeDtypeStruct((m // MOBA_BLOCK, MOBA_HEADS, nb, MOBA_BLOCK), F32)],
        scratch_shapes=[pltpu.VMEM((nb, d), F32)],
        compiler_params=_params("arbitrary"),
        name="moba_prep",
    )(qkv, c_tab, s_lo, s_hi, q_gain.reshape(1, -1), k_gain.reshape(1, -1))
    hw = MOBA_HEADS_PER_CALL * MOBA_HEAD_DIM
    attn = pl.pallas_call(
        _moba_attn_kernel,
        grid=(batch, MOBA_HEADS // MOBA_HEADS_PER_CALL, nb),
        in_specs=[pl.BlockSpec((MOBA_BLOCK, hw), lambda b, hh, i: (b * nb + i, hh)),
                  pl.BlockSpec((seq, hw), lambda b, hh, i: (b, hh)),
                  pl.BlockSpec((MOBA_HEADS_PER_CALL * MOBA_VROWS, seq), lambda b, hh, i: (hh, b)),
                  pl.BlockSpec((None, MOBA_HEADS_PER_CALL, nb, MOBA_BLOCK), lambda b, hh, i: (b * nb + i, hh, 0, 0))],
        out_specs=pl.BlockSpec((MOBA_BLOCK, hw), lambda b, hh, i: (b * nb + i, hh)),
        out_shape=jax.ShapeDtypeStruct((m, d), BF16),
        scratch_shapes=[pltpu.VMEM((MOBA_HEADS_PER_CALL, seq, MOBA_BLOCK), F32)],
        compiler_params=_params("parallel", "parallel", "parallel"),
        name="moba_attn",
    )(q, k, vt, sel)
    return matmul(attn, wo, layer=layer, residual=x, norm_gain=next_norm, tm=512)


def _head_sum_matrix():
    r = lax.broadcasted_iota(jnp.int32, (LANES, LANES), 0) // RWKV_HEAD_SIZE
    c = lax.broadcasted_iota(jnp.int32, (LANES, LANES), 1) // RWKV_HEAD_SIZE
    return jnp.where(r == c, 1.0, 0.0).astype(BF16)


def _split_dot(x, y_exact, pieces):
    out = None
    for _ in range(pieces):
        part = x.astype(BF16)
        term = jnp.dot(part, y_exact, preferred_element_type=F32)
        out = term if out is None else out + term
        x = x - part.astype(F32)
    return out


def _head_sum(x, ones_bd):
    return _split_dot(x, ones_bd, 2)


def _rwkv_mix_kernel(x_ref, h_ref, g_ref, mu_ref, *o_refs, tiles_per_seq):
    first = (pl.program_id(0) % tiles_per_seq) == 0

    def norm(x):
        return (x * lax.rsqrt(jnp.mean(x * x, axis=-1, keepdims=True) + NORM_EPS) * g_ref[...])

    h = norm(x_ref[...])
    prev_row = jnp.where(first, 0.0, norm(h_ref[...])[SUBLANES - 1:SUBLANES, :])
    row = lax.broadcasted_iota(jnp.int32, h.shape, 0)
    xx = jnp.where(row == 0, prev_row, pltpu.roll(h, 1, 0)) - h
    for idx, o_ref in enumerate(o_refs):
        o_ref[...] = (h + xx * mu_ref[idx:idx + 1, :]).astype(o_ref.dtype)


def _rwkv_prep_kernel(k_ref, wp_ref, ap_ref, w0_ref, a0_ref, kk_ref, ka_ref,
                      logw_ref, kmod_ref, an_ref, b_ref):
    ones_bd = _head_sum_matrix()
    w_log = -jax.nn.softplus(-(w0_ref[...] + wp_ref[...])) - 0.5
    logw_ref[...] = -jnp.exp(w_log)
    a = jax.nn.sigmoid(a0_ref[...] + ap_ref[...])
    k = k_ref[...].astype(F32)
    kmod_ref[...] = (k * (1.0 + (a - 1.0) * ka_ref[...])).astype(kmod_ref.dtype)
    kk = k * kk_ref[...]
    for c in range(kk.shape[1] // LANES):
        sl = slice(c * LANES, (c + 1) * LANES)
        kc = kk[:, sl]
        nrm = jnp.maximum(jnp.sqrt(_head_sum(kc * kc, ones_bd)), 1e-12)
        kc = kc / nrm
        an_ref[:, sl] = (-kc).astype(an_ref.dtype)
        b_ref[:, sl] = (kc * a[:, sl]).astype(b_ref.dtype)


def _rwkv_chunk_kernel(r_ref, lw_ref, k_ref, v_ref, a_ref, b_ref, y_ref, s_scr):
    L, n, grp = RWKV_CHUNK, RWKV_HEAD_SIZE, RWKV_GROUP
    width = grp * n
    rows = grp * L
    tb = r_ref.shape[0]

    @pl.when(pl.program_id(2) == 0)
    def _():
        s_scr[...] = jnp.zeros_like(s_scr)

    row = lax.broadcasted_iota(jnp.int32, (rows, rows), 0)
    col = lax.broadcasted_iota(jnp.int32, (rows, rows), 1)
    strict = (col % L) < (row % L)
    incl = (col % L) <= (row % L)
    lane_head = lax.broadcasted_iota(jnp.int32, (L, width), 1) // n
    cum = jnp.where(lax.broadcasted_iota(jnp.int32, (L, L), 1) <= lax.broadcasted_iota(jnp.int32, (L, L), 0),
                    1.0, 0.0).astype(BF16)

    def cumsum_rows(x):
        out = None
        for _ in range(3):
            part = x.astype(BF16)
            term = jnp.dot(cum, part, preferred_element_type=F32)
            out = term if out is None else out + term
            x = x - part.astype(F32)
        return out
    nt_dims = (((1,), (1,)), ((), ()))
    tn_dims = (((0,), (0,)), ((), ()))

    def stack(x):
        return jnp.concatenate([jnp.where(lane_head == h, x, 0.0) for h in range(grp)], axis=0)

    def unstack(xm):
        out = xm[0:L]
        for h in range(1, grp):
            out = out + xm[h * L:(h + 1) * L]
        return out

    def mm(x, y):
        return jnp.dot(x.astype(BF16), y.astype(BF16), preferred_element_type=F32)

    n_groups = r_ref.shape[1] // width
    groups = range(n_groups)

    def chunk(c, carry):
        sl = pl.ds(pl.multiple_of(c * L, L), L)
        lanes = [slice(q * width, (q + 1) * width) for q in groups]
        lw = [lw_ref[sl, ln] for ln in lanes]
        g = [cumsum_rows(x) for x in lw]
        g_last = [x[L - 1:L, :] for x in g]
        e_neg = [jnp.exp(-x) for x in g]
        r, k, v, a, b = ([ref[sl, ln].astype(F32) for ln in lanes] for ref in (r_ref, k_ref, v_ref, a_ref, b_ref))
        lhs = [jnp.concatenate([stack(a[q] * jnp.exp(g[q] - lw[q])), stack(r[q] * jnp.exp(g[q]))], axis=0).astype(BF16)
               for q in groups]
        rhs = [jnp.concatenate([stack(b[q] * e_neg[q]), stack(k[q] * e_neg[q])], axis=0).astype(BF16) for q in groups]
        v_m = [stack(x) for x in v]
        s = [s_scr[q] for q in groups]
        prod = [lax.dot_general(lhs[q], rhs[q], nt_dims, preferred_element_type=F32) for q in groups]
        from_state = [lax.dot_general(lhs[q], s[q].astype(BF16), nt_dims, preferred_element_type=F32) for q in groups]
        a_ak = [jnp.where(strict, p[:rows, rows:], 0.0) for p in prod]
        u_m = [from_state[q][:rows] + mm(a_ak[q], v_m[q]) for q in groups]
        power = [jnp.where(strict, p[:rows, :rows], 0.0) for p in prod]
        for level in range(L.bit_length() - 1):
            if level:
                power = [mm(p, p) for p in power]
            u_m = [u_m[q] + mm(power[q], u_m[q]) for q in groups]
        for q in groups:
            m_both = jnp.where(jnp.concatenate([incl, incl], axis=1), prod[q][rows:, :], 0.0).astype(BF16)
            uv = jnp.concatenate([u_m[q], v_m[q]], axis=0).astype(BF16)
            y_ref[sl, lanes[q]] = unstack(from_state[q][rows:] + jnp.dot(m_both, uv, preferred_element_type=F32))
            e_tail = jnp.exp(g_last[q] - g[q])
            tail = jnp.concatenate([stack(b[q] * e_tail), stack(k[q] * e_tail)], axis=0).astype(BF16)
            s_scr[q] = s[q] * jnp.exp(g_last[q]) + lax.dot_general(uv, tail, tn_dims, preferred_element_type=F32)
        return carry

    lax.fori_loop(0, tb // L, chunk, 0)


def _rwkv_post_kernel(y_ref, r_ref, kmod_ref, v_ref, g_ref, rk_ref, lw_ref, lb_ref, o_ref):
    ones_bd = _head_sum_matrix()
    inv_n = 1.0 / RWKV_HEAD_SIZE
    for c in range(y_ref.shape[1] // LANES):
        sl = slice(c * LANES, (c + 1) * LANES)
        y = y_ref[:, sl]
        mean = _head_sum(y, ones_bd) * inv_n
        yc = y - mean
        var = _head_sum(yc * yc, ones_bd) * inv_n
        yn = yc * lax.rsqrt(var + RWKV_GN_EPS) * lw_ref[:, sl] + lb_ref[:, sl]
        bonus = _head_sum(r_ref[:, sl].astype(F32) * kmod_ref[:, sl].astype(F32) * rk_ref[:, sl], ones_bd)
        o_ref[:, sl] = ((yn + bonus * v_ref[:, sl].astype(F32)) * g_ref[:, sl].astype(F32)).astype(o_ref.dtype)


def rwkv_layer(x, norm_g, mu, w_r, w_k, w_v, w_o, w0, w1, w2, a0, a1, a2, g1, g2,
               k_k, k_a, r_k, lnx_w, lnx_b, next_norm, batch, seq):
    m, d = x.shape
    tm = 256
    row_spec = pl.BlockSpec((tm, d), lambda i: (i, 0))
    vec_spec = pl.BlockSpec((1, d), lambda i: (0, 0))
    hb = tm // SUBLANES
    mixed = pl.pallas_call(
        functools.partial(_rwkv_mix_kernel, tiles_per_seq=seq // tm),
        grid=(m // tm,),
        in_specs=[row_spec, pl.BlockSpec((SUBLANES, d), lambda i: (jnp.maximum(i * hb - 1, 0), 0)),
                  vec_spec, pl.BlockSpec((6, d), lambda i: (0, 0))],
        out_specs=[row_spec] * 6,
        out_shape=[jax.ShapeDtypeStruct((m, d), BF16)] * 6,
        compiler_params=_params("parallel"),
        name="rwkv_mix",
    )(x, x, norm_g.reshape(1, d), mu)
    x_r, x_w, x_k, x_v, x_a, x_g = mixed

    def lora_pad(w_in, w_out):
        rank = -(-w_in.shape[1] // LANES) * LANES
        return _pad_cols(w_in, rank), _pad_rows(w_out, rank)

    r = matmul(x_r, w_r, out_dtype=BF16)
    k = matmul(x_k, w_k, out_dtype=BF16)
    v = matmul(x_v, w_v, out_dtype=BF16)
    w1p, w2p = lora_pad(w1, w2)
    a1p, a2p = lora_pad(a1, a2)
    g1p, g2p = lora_pad(g1, g2)
    w_pre = matmul(matmul(x_w, w1p, act="tanh", out_dtype=BF16), w2p)
    a_pre = matmul(matmul(x_a, a1p, out_dtype=BF16), a2p)
    gate = matmul(matmul(x_g, g1p, act="sigmoid", out_dtype=BF16), g2p, out_dtype=BF16)

    log_w, k_mod, a_neg, b_vec = pl.pallas_call(
        _rwkv_prep_kernel,
        grid=(m // tm,),
        in_specs=[row_spec] * 3 + [vec_spec] * 4,
        out_specs=[row_spec] * 4,
        out_shape=[jax.ShapeDtypeStruct((m, d), F32)] + [jax.ShapeDtypeStruct((m, d), BF16)] * 3,
        compiler_params=_params("parallel"),
        name="rwkv_prep",
    )(k, w_pre, a_pre, w0.reshape(1, d), a0.reshape(1, d), k_k.reshape(1, d), k_a.reshape(1, d))

    tb, groups_per_call = 256, 8
    width = groups_per_call * RWKV_GROUP * RWKV_HEAD_SIZE
    nt = seq // tb
    blk = pl.BlockSpec((tb, width), lambda b, p, t: (b * nt + t, p))
    y = pl.pallas_call(
        _rwkv_chunk_kernel,
        grid=(batch, d // width, nt),
        in_specs=[blk] * 6,
        out_specs=blk,
        out_shape=jax.ShapeDtypeStruct((m, d), F32),
        scratch_shapes=[pltpu.VMEM((groups_per_call, RWKV_GROUP * RWKV_HEAD_SIZE, RWKV_GROUP * RWKV_HEAD_SIZE), F32)],
        compiler_params=_params("parallel", "parallel", "arbitrary"),
        name="rwkv_chunks",
    )(r, log_w, k_mod, v, a_neg, b_vec)

    out = pl.pallas_call(
        _rwkv_post_kernel,
        grid=(m // tm,),
        in_specs=[row_spec] * 5 + [vec_spec] * 3,
        out_specs=row_spec,
        out_shape=jax.ShapeDtypeStruct((m, d), BF16),
        compiler_params=_params("parallel"),
        name="rwkv_post",
    )(y, r, k_mod, v, gate, r_k.reshape(1, d), lnx_w.reshape(1, d), lnx_b.reshape(1, d))
    return matmul(out, w_o, residual=x, norm_gain=next_norm, tm=512)


def _softcap(z):
    return GATE_SOFTCAP * jnp.tanh(z / GATE_SOFTCAP)


def _mlstm_kernel(q_ref, k_ref, v_ref, o_ref, gc_ref, gr_ref, hg_ref, out_ref, ct_scr, n_scr, m_scr):
    L, H, dk, dv = MLSTM_CHUNK, MLSTM_HEADS, MLSTM_DQK, MLSTM_DV
    heads = range(H)

    @pl.when(pl.program_id(1) == 0)
    def _():
        ct_scr[...] = jnp.zeros_like(ct_scr)
        n_scr[...] = jnp.zeros_like(n_scr)
        m_scr[...] = jnp.zeros_like(m_scr)

    t_idx = lax.broadcasted_iota(jnp.int32, (L, L), 0)
    s_idx = lax.broadcasted_iota(jnp.int32, (L, L), 1)
    causal = s_idx <= t_idx
    anti = t_idx <= s_idx
    nt_dims = (((1,), (1,)), ((), ()))

    def chunk(c, carry):
        sl = pl.ds(pl.multiple_of(c * L, L), L)
        gc, gr = gc_ref[sl, :], gr_ref[c]
        li_cols, li_rows = _softcap(gc[:, :H]), _softcap(gr[:H, :])
        lf_cols = jax.nn.log_sigmoid(_softcap(gc[:, H:]))
        lf_rows = jax.nn.log_sigmoid(_softcap(gr[H:, :]))
        li_col = [li_cols[:, h:h + 1] for h in heads]
        li_row = [li_rows[h:h + 1, :] for h in heads]
        lf_col = [lf_cols[:, h:h + 1] for h in heads]
        lf_row = [lf_rows[h:h + 1, :] for h in heads]
        b_col = [jnp.sum(jnp.where(causal, lf_row[h], 0.0), axis=1, keepdims=True) for h in heads]
        b_row = [jnp.sum(jnp.where(anti, lf_col[h], 0.0), axis=0, keepdims=True) for h in heads]
        b_last = [jnp.sum(lf_row[h], axis=1, keepdims=True) for h in heads]
        m_prev = [m_scr[h] for h in heads]
        dmat = [jnp.where(causal, b_col[h] - b_row[h] + li_row[h], NEG_BIG) for h in heads]
        inter = [b_col[h] + m_prev[h] for h in heads]
        m_t = [jnp.maximum(inter[h], jnp.max(dmat[h], axis=1, keepdims=True)) for h in heads]
        q = [q_ref[sl, h * dk:(h + 1) * dk].astype(F32) for h in heads]
        k = [k_ref[sl, h * dk:(h + 1) * dk].astype(F32) * (dk ** -0.5) for h in heads]
        vb = [v_ref[sl, h * dv:(h + 1) * dv].astype(BF16) for h in heads]
        qb = [x.astype(BF16) for x in q]
        s = [lax.dot_general(qb[h], k[h].astype(BF16), nt_dims, preferred_element_type=F32) * jnp.exp(dmat[h] - m_t[h])
             for h in heads]
        w_inter = [jnp.exp(inter[h] - m_t[h]) for h in heads]
        ct = [ct_scr[h] for h in heads]
        n_row = [n_scr[h] for h in heads]
        num = [jnp.dot(s[h].astype(BF16), vb[h], preferred_element_type=F32)
               + w_inter[h] * jnp.dot(qb[h], ct[h].astype(BF16), preferred_element_type=F32) for h in heads]
        den = [jnp.sum(s[h], axis=1, keepdims=True) + w_inter[h] * jnp.sum(q[h] * n_row[h], axis=1, keepdims=True)
               for h in heads]
        h_c = [num[h] / jnp.maximum(jnp.abs(den[h]), jnp.exp(-m_t[h])) for h in heads]
        for h in heads:
            hn = h_c[h] * lax.rsqrt(jnp.mean(h_c[h] * h_c[h], axis=-1, keepdims=True) + NORM_EPS)
            cols = slice(h * dv, (h + 1) * dv)
            out_ref[sl, cols] = (hn * hg_ref[:, cols] * jax.nn.sigmoid(o_ref[sl, cols].astype(F32))).astype(out_ref.dtype)
        d_row = [b_last[h] - b_row[h] + li_row[h] for h in heads]
        m_new = [jnp.maximum(b_last[h] + m_prev[h], jnp.max(d_row[h], axis=1, keepdims=True)) for h in heads]
        kw = [k[h] * jnp.exp(b_last[h] - b_col[h] + li_col[h] - m_new[h]) for h in heads]
        w_c = [jnp.exp(b_last[h] + m_prev[h] - m_new[h]) for h in heads]
        for h in heads:
            ct_scr[h] = w_c[h] * ct[h] + jnp.dot(kw[h].T.astype(BF16), vb[h], preferred_element_type=F32)
            n_scr[h] = w_c[h] * n_row[h] + jnp.sum(kw[h], axis=0, keepdims=True)
            m_scr[h] = m_new[h]
        return carry

    lax.fori_loop(0, q_ref.shape[0] // L, chunk, 0)


def mlstm_layer(x, norm_g, w_in, b_if, head_gain, w_out, next_norm, batch, seq):
    m, d = x.shape
    H, L, dk, dv = MLSTM_HEADS, MLSTM_CHUNK, MLSTM_DQK, MLSTM_DV
    tb = 4 * L
    nt = seq // tb
    h = rmsnorm(x, norm_g)
    n_main = 2 * H * dk + 2 * H * dv
    proj = matmul(h, w_in, n=n_main, out_dtype=BF16)
    gates = matmul(h, _pad_cols(w_in[:, n_main:], LANES), bias=_pad_cols(b_if.reshape(1, 2 * H), LANES))
    gates = gates[:, :2 * H]
    gates_t = jnp.transpose(gates.reshape(m // L, L, 2 * H), (0, 2, 1))
    out = pl.pallas_call(
        _mlstm_kernel,
        grid=(batch, nt),
        in_specs=[pl.BlockSpec((tb, H * dk), lambda b, t: (b * nt + t, 0)),
                  pl.BlockSpec((tb, H * dk), lambda b, t: (b * nt + t, 1)),
                  pl.BlockSpec((tb, H * dv), lambda b, t: (b * nt + t, 1)),
                  pl.BlockSpec((tb, H * dv), lambda b, t: (b * nt + t, 2)),
                  pl.BlockSpec((tb, 2 * H), lambda b, t: (b * nt + t, 0)),
                  pl.BlockSpec((tb // L, 2 * H, L), lambda b, t: (b * nt + t, 0, 0)),
                  pl.BlockSpec((1, H * dv), lambda b, t: (0, 0))],
        out_specs=pl.BlockSpec((tb, H * dv), lambda b, t: (b * nt + t, 0)),
        out_shape=jax.ShapeDtypeStruct((m, H * dv), BF16),
        scratch_shapes=[pltpu.VMEM((H, dk, dv), F32), pltpu.VMEM((H, 1, dk), F32), pltpu.VMEM((H, 1, 1), F32)],
        compiler_params=_params("parallel", "arbitrary"),
        name="mlstm_chunks",
    )(proj, proj, proj, proj, gates, gates_t, head_gain.reshape(1, -1))
    return matmul(out, w_out, residual=x, norm_gain=next_norm, tm=512)


def kernel(x, moba_norm, moba_wqkv, moba_q_gain, moba_k_gain, moba_wo, rwkv_norm, rwkv_mu, rwkv_w_r, rwkv_w_k, rwkv_w_v, rwkv_w_o, rwkv_w0, rwkv_w1, rwkv_w2, rwkv_a0, rwkv_a1, rwkv_a2, rwkv_g1, rwkv_g2, rwkv_k_k, rwkv_k_a, rwkv_r_k, rwkv_lnx_w, rwkv_lnx_b, mlstm_norm, mlstm_w_in, mlstm_b_if, mlstm_head_gain, mlstm_w_out, ffn_norm, ffn_w_up, ffn_conv_w, ffn_conv_b, ffn_w_down):
    batch, seq, d = x.shape
    depth = ffn_norm.shape[0]
    x = x.reshape(batch * seq, d)
    for i in range(depth):
        kind, j = i % 3, i // 3
        if kind == 0:
            x, h = moba_layer(x, moba_norm[j], moba_wqkv, moba_q_gain[j], moba_k_gain[j], moba_wo, j,
                              ffn_norm[i], batch, seq)
        elif kind == 1:
            x, h = rwkv_layer(x, rwkv_norm[j], rwkv_mu[j], rwkv_w_r[j], rwkv_w_k[j], rwkv_w_v[j], rwkv_w_o[j],
                           rwkv_w0[j], rwkv_w1[j], rwkv_w2[j], rwkv_a0[j], rwkv_a1[j], rwkv_a2[j],
                           rwkv_g1[j], rwkv_g2[j], rwkv_k_k[j], rwkv_k_a[j], rwkv_r_k[j],
                           rwkv_lnx_w[j], rwkv_lnx_b[j], ffn_norm[i], batch, seq)
        else:
            x, h = mlstm_layer(x, mlstm_norm[j], mlstm_w_in[j], mlstm_b_if[j], mlstm_head_gain[j],
                               mlstm_w_out[j], ffn_norm[i], batch, seq)
        x = conv_ffn(x, h, ffn_w_up, ffn_conv_w, ffn_conv_b, ffn_w_down, i, seq)
    return x.reshape(batch, seq, d)
```

```python
import functools

import jax
import jax.numpy as jnp
from jax import lax
from jax.experimental import pallas as pl
from jax.experimental.pallas import tpu as pltpu

F32 = jnp.float32
BF16 = jnp.bfloat16

LANES = 128
SUBLANES = 8
VMEM_LIMIT_BYTES = 56 * 1024 * 1024

NORM_EPS = 1e-6
MOBA_HEADS = 16
MOBA_HEAD_DIM = 128
MOBA_BLOCK = 256
MOBA_TOPK = 3
MOBA_HEADS_PER_CALL = 4
MOBA_VROWS = MOBA_HEAD_DIM + 16
MOBA_Q_SCALE = 1.4426950408889634 * MOBA_HEAD_DIM ** -0.5
ROPE_THETA = 500000.0
ROT_DIM = MOBA_HEAD_DIM // 4
RWKV_HEAD_SIZE = 64
RWKV_GN_EPS = 64e-5
RWKV_DECAY_SCALE = 0.6065306597126334
RWKV_CHUNK = 64
RWKV_GROUP = 4
MLSTM_HEADS = 8
MLSTM_DV = 256
MLSTM_DQK = 128
MLSTM_CHUNK = 64
GATE_SOFTCAP = 15.0
CONV_WIDTH = 3
FFN_HALO = 16
NEG_BIG = -1e30

HI = lax.Precision.HIGHEST


def _params(*semantics):
    return pltpu.CompilerParams(dimension_semantics=semantics, vmem_limit_bytes=VMEM_LIMIT_BYTES)


def _pad_cols(w, n):
    return jnp.pad(w, ((0, 0), (0, n - w.shape[1])))


def _pad_rows(w, n):
    return jnp.pad(w, ((0, n - w.shape[0]), (0, 0)))


def _rmsnorm_kernel(x_ref, g_ref, o_ref):
    x = x_ref[...]
    y = x * lax.rsqrt(jnp.mean(x * x, axis=-1, keepdims=True) + NORM_EPS)
    o_ref[...] = (y * g_ref[...]).astype(o_ref.dtype)


def rmsnorm(x, gain, tm=512):
    m, d = x.shape
    return pl.pallas_call(
        _rmsnorm_kernel,
        grid=(m // tm,),
        in_specs=[pl.BlockSpec((tm, d), lambda i: (i, 0)), pl.BlockSpec((1, d), lambda i: (0, 0))],
        out_specs=pl.BlockSpec((tm, d), lambda i: (i, 0)),
        out_shape=jax.ShapeDtypeStruct((m, d), BF16),
        compiler_params=_params("parallel"),
        name="rmsnorm",
    )(x, gain.reshape(1, d))


def _mm_kernel(*refs, act, has_bias, has_res, has_norm, cast_w):
    x_ref, w_ref = refs[0], refs[1]
    if cast_w:
        w_ref, refs = refs[-1], refs[:-1]

        @pl.when(pl.program_id(1) == 0)
        def _():
            w_ref[...] = refs[1][...].astype(BF16)

    acc = jnp.dot(x_ref[...], w_ref[...], preferred_element_type=F32)
    k = 2
    if has_bias:
        acc = acc + refs[k][...]
        k += 1
    if act == "tanh":
        acc = jnp.tanh(acc)
    elif act == "sigmoid":
        acc = jax.nn.sigmoid(acc)
    if has_res:
        acc = acc + refs[k][...]
        k += 1
    if has_norm:
        o_ref, n_ref = refs[-2], refs[-1]
        o_ref[...] = acc.astype(o_ref.dtype)
        y = acc * lax.rsqrt(jnp.mean(acc * acc, axis=-1, keepdims=True) + NORM_EPS)
        n_ref[...] = (y * refs[k][...]).astype(n_ref.dtype)
    else:
        refs[-1][...] = acc.astype(refs[-1].dtype)


def matmul(x, w, *, layer=None, n=None, bias=None, act=None, residual=None, norm_gain=None, out_dtype=F32,
           tm=1024, tn=1024):
    m, kdim = x.shape
    n = w.shape[-1] if n is None else n
    tm, tn = min(tm, m), min(tn, n)
    if norm_gain is not None:
        tn = n
    assert m % tm == 0 and n % tn == 0, (m, n, tm, tn)
    cast_w = w.dtype != BF16
    if layer is None:
        w_spec = pl.BlockSpec((kdim, tn), lambda j, i: (0, j))
    else:
        w_spec = pl.BlockSpec((None, kdim, tn), lambda j, i: (layer, 0, j))
    in_specs = [pl.BlockSpec((tm, kdim), lambda j, i: (i, 0)), w_spec]
    args = [x, w]
    if bias is not None:
        in_specs.append(pl.BlockSpec((1, tn), lambda j, i: (0, j)))
        args.append(bias.reshape(1, n))
    if residual is not None:
        in_specs.append(pl.BlockSpec((tm, tn), lambda j, i: (i, j)))
        args.append(residual)
    out_spec = pl.BlockSpec((tm, tn), lambda j, i: (i, j))
    out_specs, out_shape = out_spec, jax.ShapeDtypeStruct((m, n), out_dtype)
    if norm_gain is not None:
        in_specs.append(pl.BlockSpec((1, tn), lambda j, i: (0, j)))
        args.append(norm_gain.reshape(1, n))
        out_specs, out_shape = [out_spec, out_spec], [out_shape, jax.ShapeDtypeStruct((m, n), BF16)]
    return pl.pallas_call(
        functools.partial(_mm_kernel, act=act, has_bias=bias is not None, has_res=residual is not None,
                          has_norm=norm_gain is not None, cast_w=cast_w),
        grid=(n // tn, m // tm),
        in_specs=in_specs,
        out_specs=out_specs,
        out_shape=out_shape,
        scratch_shapes=[pltpu.VMEM((kdim, tn), BF16)] if cast_w else [],
        compiler_params=_params("parallel", "arbitrary"),
        name="matmul",
    )(*args)


def _ffn_up_kernel(x_ref, h_ref, wg_ref, wu_ref, cwg_ref, cwu_ref, cbg_ref, cbu_ref, o_ref, wgb_scr, wub_scr,
                   *, tiles_per_seq):
    i = pl.program_id(1)
    first = (i % tiles_per_seq) == 0

    @pl.when(i == 0)
    def _():
        wgb_scr[...] = wg_ref[...].astype(BF16)
        wub_scr[...] = wu_ref[...].astype(BF16)

    halo = jnp.where(first, jnp.zeros_like(h_ref[...]), h_ref[...])
    x = jnp.concatenate([halo, x_ref[...]], axis=0)

    def branch(w_scr, cw_ref, cb_ref):
        u = jnp.dot(x, w_scr[...], preferred_element_type=F32)
        cw = cw_ref[...]
        return pltpu.roll(u, 2, 0) * cw[0:1, :] + pltpu.roll(u, 1, 0) * cw[1:2, :] + u * cw[2:3, :] + cb_ref[...]

    gate = branch(wgb_scr, cwg_ref, cbg_ref)
    act = gate * jax.nn.sigmoid(gate) * branch(wub_scr, cwu_ref, cbu_ref)
    o_ref[...] = act[FFN_HALO:].astype(o_ref.dtype)


def ffn_up(h, w_up, conv_w, conv_b, layer, seq, tm=1024, tn=512):
    m, d = h.shape
    f = w_up.shape[-1] // 2
    nf = f // tn
    assert f % tn == 0 and seq % tm == 0
    hb = tm // FFN_HALO
    return pl.pallas_call(
        functools.partial(_ffn_up_kernel, tiles_per_seq=seq // tm),
        grid=(nf, m // tm),
        in_specs=[
            pl.BlockSpec((tm, d), lambda c, i: (i, 0)),
            pl.BlockSpec((FFN_HALO, d), lambda c, i: (jnp.maximum(i * hb - 1, 0), 0)),
            pl.BlockSpec((None, d, tn), lambda c, i: (layer, 0, c)),
            pl.BlockSpec((None, d, tn), lambda c, i: (layer, 0, c + nf)),
            pl.BlockSpec((None, CONV_WIDTH, tn), lambda c, i: (layer, 0, c)),
            pl.BlockSpec((None, CONV_WIDTH, tn), lambda c, i: (layer, 0, c + nf)),
            pl.BlockSpec((None, 1, tn), lambda c, i: (layer, 0, c)),
            pl.BlockSpec((None, 1, tn), lambda c, i: (layer, 0, c + nf)),
        ],
        out_specs=pl.BlockSpec((tm, tn), lambda c, i: (i, c)),
        out_shape=jax.ShapeDtypeStruct((m, f), BF16),
        scratch_shapes=[pltpu.VMEM((d, tn), BF16), pltpu.VMEM((d, tn), BF16)],
        compiler_params=_params("parallel", "arbitrary"),
        name="ffn_up_conv",
    )(h, h, w_up, w_up, conv_w, conv_w, conv_b[:, None, :], conv_b[:, None, :])


def conv_ffn(x, h, w_up, conv_w, conv_b, w_down, layer, seq):
    act = ffn_up(h, w_up, conv_w, conv_b, layer, seq)
    return matmul(act, w_down, layer=layer, residual=x, tm=512, tn=512)


def _rope_tables(seq):
    half = ROT_DIM // 2
    inv_freq = jnp.float32(ROPE_THETA) ** (-jnp.arange(0, ROT_DIM, 2, dtype=F32) / ROT_DIM)
    ang = jnp.arange(seq, dtype=F32)[:, None] * inv_freq[None, :]
    cos, sin = jnp.cos(ang), jnp.sin(ang)
    ones = jnp.ones((seq, MOBA_HEAD_DIM - ROT_DIM), F32)
    zeros_r = jnp.zeros((seq, MOBA_HEAD_DIM - ROT_DIM), F32)
    c_tab = jnp.concatenate([cos, cos, ones], axis=1)
    s_tab = jnp.concatenate([sin, sin, zeros_r], axis=1)
    return c_tab, s_tab


def _moba_prep_kernel(qkv_ref, c_ref, s_ref, qg_ref, kg_ref,
                      q_ref, k_ref, vt_ref, sel_ref, kmean_scr, *, blocks_per_seq):
    n = pl.program_id(0) % blocks_per_seq
    d = MOBA_HEADS * MOBA_HEAD_DIM
    half = ROT_DIM // 2
    c_tab, s_tab = c_ref[...], s_ref[...]
    mean_sq = jnp.full((MOBA_HEAD_DIM, MOBA_HEAD_DIM), 1.0 / MOBA_HEAD_DIM, BF16)
    src = lax.broadcasted_iota(jnp.int32, (MOBA_HEAD_DIM, MOBA_HEAD_DIM), 0)
    dst = lax.broadcasted_iota(jnp.int32, (MOBA_HEAD_DIM, MOBA_HEAD_DIM), 1)
    rot = jnp.where((dst < half) & (src == dst + half), -1.0,
                    jnp.where((dst >= half) & (dst < 2 * half) & (src == dst - half), 1.0, 0.0)).astype(BF16)

    @pl.when(n == 0)
    def _():
        kmean_scr[...] = jnp.zeros_like(kmean_scr)

    def norm_rope(x, gain):
        y = x * lax.rsqrt(jnp.dot((x * x).astype(BF16), mean_sq, preferred_element_type=F32) + NORM_EPS) * gain
        return y * c_tab + _split_dot(y, rot, 2) * s_tab

    nb = sel_ref.shape[1]
    blk = lax.broadcasted_iota(jnp.int32, (nb, MOBA_BLOCK), 0)
    eligible = blk < n
    for h in range(MOBA_HEADS):
        lo = h * MOBA_HEAD_DIM
        q = norm_rope(qkv_ref[:, lo:lo + MOBA_HEAD_DIM].astype(F32), qg_ref[...])
        k = norm_rope(qkv_ref[:, d + lo:d + lo + MOBA_HEAD_DIM].astype(F32), kg_ref[...])
        q_ref[:, lo:lo + MOBA_HEAD_DIM] = (q * MOBA_Q_SCALE).astype(q_ref.dtype)
        k_ref[:, lo:lo + MOBA_HEAD_DIM] = k.astype(k_ref.dtype)
        vlo = h * MOBA_VROWS
        vt_ref[vlo:vlo + MOBA_HEAD_DIM, :] = qkv_ref[:, 2 * d + lo:2 * d + lo + MOBA_HEAD_DIM].astype(F32).T.astype(vt_ref.dtype)
        vt_ref[vlo + MOBA_HEAD_DIM:vlo + MOBA_VROWS, :] = jnp.ones((MOBA_VROWS - MOBA_HEAD_DIM, MOBA_BLOCK), vt_ref.dtype)
        gate = lax.dot_general(kmean_scr[:, lo:lo + MOBA_HEAD_DIM], q, (((1,), (1,)), ((), ())),
                               precision=HI, preferred_element_type=F32)
        gate = jnp.where(eligible, gate, -jnp.inf)
        sel = jnp.zeros((nb, MOBA_BLOCK), F32)
        for j in range(nb):
            gj = gate[j:j + 1, :]
            beats = jnp.where(gate > gj, 1.0, jnp.where((gate == gj) & (blk < j), 1.0, 0.0))
            rank = jnp.sum(beats, axis=0, keepdims=True)
            chosen = jnp.where((rank < MOBA_TOPK) & (j < n), 1.0, 0.0)
            sel = jnp.where(blk == j, chosen, sel)
        sel_ref[h] = (sel - 1.0) * -NEG_BIG
        km_rows = lax.broadcasted_iota(jnp.int32, (nb, MOBA_HEAD_DIM), 0)
        kmean_scr[:, lo:lo + MOBA_HEAD_DIM] = jnp.where(km_rows == n, jnp.mean(k, axis=0, keepdims=True),
                                                        kmean_scr[:, lo:lo + MOBA_HEAD_DIM])


def _moba_attn_kernel(q_ref, k_ref, vt_ref, bias_ref, o_ref, s_scr):
    i = pl.program_id(2)
    heads = range(MOBA_HEADS_PER_CALL)
    lanes = [slice(h * MOBA_HEAD_DIM, (h + 1) * MOBA_HEAD_DIM) for h in heads]
    vrows = [slice(h * MOBA_VROWS, (h + 1) * MOBA_VROWS) for h in heads]
    q = [q_ref[:, ln] for ln in lanes]
    nt_dims = (((1,), (1,)), ((), ()))
    pair = 2 * MOBA_BLOCK
    n_pairs = (i + 2) // 2
    kpos = lax.broadcasted_iota(jnp.int32, (MOBA_BLOCK, MOBA_BLOCK), 0)
    qpos = lax.broadcasted_iota(jnp.int32, (MOBA_BLOCK, MOBA_BLOCK), 1)
    causal_bias = jnp.where(kpos <= qpos, 0.0, NEG_BIG)

    def store_scores(p, m_run):
        start = pl.multiple_of(p * pair, pair)
        m_new = []
        for h in heads:
            s = lax.dot_general(k_ref[pl.ds(start, pair), lanes[h]], q[h], nt_dims, preferred_element_type=F32)
            bias = jnp.concatenate([jnp.where(2 * p + e == i, causal_bias, bias_ref[h, pl.ds(2 * p + e, 1), :])
                                    for e in range(2)], axis=0)
            s = s + bias
            s_scr[h, pl.ds(start, pair), :] = s
            m_new.append(jnp.maximum(m_run[h], jnp.max(s.reshape(pair // SUBLANES, SUBLANES, MOBA_BLOCK), axis=0)))
        return tuple(m_new)

    m_part = lax.fori_loop(0, n_pairs, store_scores,
                           tuple(jnp.full((SUBLANES, MOBA_BLOCK), NEG_BIG, F32) for _ in heads))
    m_fin = [jnp.max(x, axis=0, keepdims=True) for x in m_part]

    def accumulate(p, acc):
        start = pl.multiple_of(p * pair, pair)
        return tuple(
            acc[h] + jnp.dot(vt_ref[vrows[h], pl.ds(start, pair)],
                             jnp.exp2(s_scr[h, pl.ds(start, pair), :] - m_fin[h]).astype(BF16),
                             preferred_element_type=F32)
            for h in heads)

    acc = lax.fori_loop(0, n_pairs, accumulate,
                        tuple(jnp.zeros((MOBA_VROWS, MOBA_BLOCK), F32) for _ in heads))
    for h in heads:
        out_t = acc[h][:MOBA_HEAD_DIM] / acc[h][MOBA_HEAD_DIM:MOBA_HEAD_DIM + 1]
        o_ref[:, lanes[h]] = out_t.T.astype(o_ref.dtype)


def moba_layer(x, norm_g, wqkv, q_gain, k_gain, wo, layer, next_norm, batch, seq):
    m, d = x.shape
    nb = seq // MOBA_BLOCK
    h = rmsnorm(x, norm_g)
    qkv = matmul(h, wqkv, layer=layer, out_dtype=BF16)
    c_tab, s_tab = _rope_tables(seq)
    tab_spec = pl.BlockSpec((MOBA_BLOCK, MOBA_HEAD_DIM), lambda i: (i % nb, 0))
    gain_spec = pl.BlockSpec((1, MOBA_HEAD_DIM), lambda i: (0, 0))
    q, k, vt, sel = pl.pallas_call(
        functools.partial(_moba_prep_kernel, blocks_per_seq=nb),
        grid=(m // MOBA_BLOCK,),
        in_specs=[pl.BlockSpec((MOBA_BLOCK, 3 * d), lambda i: (i, 0)), tab_spec, tab_spec, gain_spec, gain_spec],
        out_specs=[pl.BlockSpec((MOBA_BLOCK, d), lambda i: (i, 0)),
                   pl.BlockSpec((MOBA_BLOCK, d), lambda i: (i, 0)),
                   pl.BlockSpec((MOBA_HEADS * MOBA_VROWS, MOBA_BLOCK), lambda i: (0, i)),
                   pl.BlockSpec((None, MOBA_HEADS, nb, MOBA_BLOCK), lambda i: (i, 0, 0, 0))],
        out_shape=[jax.ShapeDtypeStruct((m, d), BF16), jax.ShapeDtypeStruct((m, d), BF16),
                   jax.ShapeDtypeStruct((MOBA_HEADS * MOBA_VROWS, m), BF16),
                   jax.ShapeDtypeStruct((m // MOBA_BLOCK, MOBA_HEADS, nb, MOBA_BLOCK), F32)],
        scratch_shapes=[pltpu.VMEM((nb, d), F32)],
        compiler_params=_params("arbitrary"),
        name="moba_prep",
    )(qkv, c_tab, s_tab, q_gain.reshape(1, -1), k_gain.reshape(1, -1))
    hw = MOBA_HEADS_PER_CALL * MOBA_HEAD_DIM
    attn = pl.pallas_call(
        _moba_attn_kernel,
        grid=(batch, MOBA_HEADS // MOBA_HEADS_PER_CALL, nb),
        in_specs=[pl.BlockSpec((MOBA_BLOCK, hw), lambda b, hh, i: (b * nb + i, hh)),
                  pl.BlockSpec((seq, hw), lambda b, hh, i: (b, hh)),
                  pl.BlockSpec((MOBA_HEADS_PER_CALL * MOBA_VROWS, seq), lambda b, hh, i: (hh, b)),
                  pl.BlockSpec((None, MOBA_HEADS_PER_CALL, nb, MOBA_BLOCK), lambda b, hh, i: (b * nb + i, hh, 0, 0))],
        out_specs=pl.BlockSpec((MOBA_BLOCK, hw), lambda b, hh, i: (b * nb + i, hh)),
        out_shape=jax.ShapeDtypeStruct((m, d), BF16),
        scratch_shapes=[pltpu.VMEM((MOBA_HEADS_PER_CALL, seq, MOBA_BLOCK), F32)],
        compiler_params=_params("parallel", "parallel", "parallel"),
        name="moba_attn",
    )(q, k, vt, sel)
    return matmul(attn, wo, layer=layer, residual=x, norm_gain=next_norm, tm=512)


def _head_sum_matrix():
    r = lax.broadcasted_iota(jnp.int32, (LANES, LANES), 0) // RWKV_HEAD_SIZE
    c = lax.broadcasted_iota(jnp.int32, (LANES, LANES), 1) // RWKV_HEAD_SIZE
    return jnp.where(r == c, 1.0, 0.0).astype(BF16)


def _split_dot(x, y_exact, pieces):
    out = None
    for _ in range(pieces):
        part = x.astype(BF16)
        term = jnp.dot(part, y_exact, preferred_element_type=F32)
        out = term if out is None else out + term
        x = x - part.astype(F32)
    return out


def _head_sum(x, ones_bd):
    return _split_dot(x, ones_bd, 2)


def _rwkv_mix_kernel(x_ref, h_ref, g_ref, mu_ref, *o_refs, tiles_per_seq):
    first = (pl.program_id(0) % tiles_per_seq) == 0

    def norm(x):
        return (x * lax.rsqrt(jnp.mean(x * x, axis=-1, keepdims=True) + NORM_EPS) * g_ref[...])

    h = norm(x_ref[...])
    prev_row = jnp.where(first, 0.0, norm(h_ref[...])[SUBLANES - 1:SUBLANES, :])
    row = lax.broadcasted_iota(jnp.int32, h.shape, 0)
    xx = jnp.where(row == 0, prev_row, pltpu.roll(h, 1, 0)) - h
    for idx, o_ref in enumerate(o_refs):
        o_ref[...] = (h + xx * mu_ref[idx:idx + 1, :]).astype(o_ref.dtype)


def _rwkv_prep_kernel(k_ref, tw_ref, ta_ref, w2_ref, a2_ref, w0_ref, a0_ref, kk_ref, ka_ref,
                      logw_ref, kmod_ref, an_ref, b_ref):
    ones_bd = _head_sum_matrix()
    w_pre = jnp.dot(tw_ref[...], w2_ref[...], preferred_element_type=F32)
    a_pre = jnp.dot(ta_ref[...], a2_ref[...], preferred_element_type=F32)
    logw_ref[...] = -RWKV_DECAY_SCALE * jax.nn.sigmoid(w0_ref[...] + w_pre)
    a = jax.nn.sigmoid(a0_ref[...] + a_pre)
    k = k_ref[...].astype(F32)
    kmod_ref[...] = (k * (1.0 + (a - 1.0) * ka_ref[...])).astype(kmod_ref.dtype)
    kk = k * kk_ref[...]
    for c in range(kk.shape[1] // LANES):
        sl = slice(c * LANES, (c + 1) * LANES)
        kc = kk[:, sl]
        nrm = jnp.maximum(jnp.sqrt(_head_sum(kc * kc, ones_bd)), 1e-12)
        kc = kc / nrm
        an_ref[:, sl] = (-kc).astype(an_ref.dtype)
        b_ref[:, sl] = (kc * a[:, sl]).astype(b_ref.dtype)


def _rwkv_chunk_kernel(r_ref, lw_ref, k_ref, v_ref, a_ref, b_ref, y_ref, s_scr):
    L, n, grp = RWKV_CHUNK, RWKV_HEAD_SIZE, RWKV_GROUP
    width = grp * n
    rows = grp * L
    tb = r_ref.shape[0]

    @pl.when(pl.program_id(2) == 0)
    def _():
        s_scr[...] = jnp.zeros_like(s_scr)

    row = lax.broadcasted_iota(jnp.int32, (rows, rows), 0)
    col = lax.broadcasted_iota(jnp.int32, (rows, rows), 1)
    strict = (col % L) < (row % L)
    incl = (col % L) <= (row % L)
    lane_head = lax.broadcasted_iota(jnp.int32, (L, width), 1) // n
    cum = jnp.where(lax.broadcasted_iota(jnp.int32, (L, L), 1) <= lax.broadcasted_iota(jnp.int32, (L, L), 0),
                    1.0, 0.0).astype(BF16)

    def cumsum_rows(x):
        out = None
        for _ in range(3):
            part = x.astype(BF16)
            term = jnp.dot(cum, part, preferred_element_type=F32)
            out = term if out is None else out + term
            x = x - part.astype(F32)
        return out
    nt_dims = (((1,), (1,)), ((), ()))
    tn_dims = (((0,), (0,)), ((), ()))

    def stack(x):
        return jnp.concatenate([jnp.where(lane_head == h, x, 0.0) for h in range(grp)], axis=0)

    def unstack(xm):
        out = xm[0:L]
        for h in range(1, grp):
            out = out + xm[h * L:(h + 1) * L]
        return out

    def mm(x, y):
        return jnp.dot(x.astype(BF16), y.astype(BF16), preferred_element_type=F32)

    n_groups = r_ref.shape[1] // width
    groups = range(n_groups)

    def chunk(c, carry):
        sl = pl.ds(pl.multiple_of(c * L, L), L)
        lanes = [slice(q * width, (q + 1) * width) for q in groups]
        lw = [lw_ref[sl, ln] for ln in lanes]
        g = [cumsum_rows(x) for x in lw]
        g_last = [x[L - 1:L, :] for x in g]
        e_neg = [jnp.exp(-x) for x in g]
        r, k, v, a, b = ([ref[sl, ln].astype(F32) for ln in lanes] for ref in (r_ref, k_ref, v_ref, a_ref, b_ref))
        lhs = [jnp.concatenate([stack(a[q] * jnp.exp(g[q] - lw[q])), stack(r[q] * jnp.exp(g[q]))], axis=0).astype(BF16)
               for q in groups]
        rhs = [jnp.concatenate([stack(b[q] * e_neg[q]), stack(k[q] * e_neg[q])], axis=0).astype(BF16) for q in groups]
        v_m = [stack(x) for x in v]
        s = [s_scr[q] for q in groups]
        prod = [lax.dot_general(lhs[q], rhs[q], nt_dims, preferred_element_type=F32) for q in groups]
        from_state = [lax.dot_general(lhs[q], s[q].astype(BF16), nt_dims, preferred_element_type=F32) for q in groups]
        a_ak = [jnp.where(strict, p[:rows, rows:], 0.0) for p in prod]
        u_m = [from_state[q][:rows] + mm(a_ak[q], v_m[q]) for q in groups]
        power = [jnp.where(strict, p[:rows, :rows], 0.0) for p in prod]
        for level in range(L.bit_length() - 1):
            if level:
                power = [mm(p, p) for p in power]
            u_m = [u_m[q] + mm(power[q], u_m[q]) for q in groups]
        for q in groups:
            m_both = jnp.where(jnp.concatenate([incl, incl], axis=1), prod[q][rows:, :], 0.0).astype(BF16)
            uv = jnp.concatenate([u_m[q], v_m[q]], axis=0).astype(BF16)
            y_ref[sl, lanes[q]] = unstack(from_state[q][rows:] + jnp.dot(m_both, uv, preferred_element_type=F32))
            e_tail = jnp.exp(g_last[q] - g[q])
            tail = jnp.concatenate([stack(b[q] * e_tail), stack(k[q] * e_tail)], axis=0).astype(BF16)
            s_scr[q] = s[q] * jnp.exp(g_last[q]) + lax.dot_general(uv, tail, tn_dims, preferred_element_type=F32)
        return carry

    lax.fori_loop(0, tb // L, chunk, 0)


def _rwkv_post_kernel(y_ref, r_ref, kmod_ref, v_ref, tg_ref, g2_ref, rk_ref, lw_ref, lb_ref, o_ref):
    ones_bd = _head_sum_matrix()
    inv_n = 1.0 / RWKV_HEAD_SIZE
    gate = jnp.dot(tg_ref[...], g2_ref[...], preferred_element_type=F32)
    for c in range(y_ref.shape[1] // LANES):
        sl = slice(c * LANES, (c + 1) * LANES)
        y = y_ref[:, sl]
        mean = _head_sum(y, ones_bd) * inv_n
        yc = y - mean
        var = _head_sum(yc * yc, ones_bd) * inv_n
        yn = yc * lax.rsqrt(var + RWKV_GN_EPS) * lw_ref[:, sl] + lb_ref[:, sl]
        bonus = _head_sum(r_ref[:, sl].astype(F32) * kmod_ref[:, sl].astype(F32) * rk_ref[:, sl], ones_bd)
        o_ref[:, sl] = ((yn + bonus * v_ref[:, sl].astype(F32)) * gate[:, sl]).astype(o_ref.dtype)


def rwkv_layer(x, norm_g, mu, w_r, w_k, w_v, w_o, w0, w1, w2, a0, a1, a2, g1, g2,
               k_k, k_a, r_k, lnx_w, lnx_b, next_norm, batch, seq):
    m, d = x.shape
    tm = 256
    row_spec = pl.BlockSpec((tm, d), lambda i: (i, 0))
    vec_spec = pl.BlockSpec((1, d), lambda i: (0, 0))
    hb = tm // SUBLANES
    mixed = pl.pallas_call(
        functools.partial(_rwkv_mix_kernel, tiles_per_seq=seq // tm),
        grid=(m // tm,),
        in_specs=[row_spec, pl.BlockSpec((SUBLANES, d), lambda i: (jnp.maximum(i * hb - 1, 0), 0)),
                  vec_spec, pl.BlockSpec((6, d), lambda i: (0, 0))],
        out_specs=[row_spec] * 6,
        out_shape=[jax.ShapeDtypeStruct((m, d), BF16)] * 6,
        compiler_params=_params("parallel"),
        name="rwkv_mix",
    )(x, x, norm_g.reshape(1, d), mu)
    x_r, x_w, x_k, x_v, x_a, x_g = mixed

    def lora_pad(w_in, w_out):
        rank = -(-w_in.shape[1] // LANES) * LANES
        return _pad_cols(w_in, rank), _pad_rows(w_out, rank)

    r = matmul(x_r, w_r, out_dtype=BF16)
    k = matmul(x_k, w_k, out_dtype=BF16)
    v = matmul(x_v, w_v, out_dtype=BF16)
    w1p, w2p = lora_pad(w1, w2)
    a1p, a2p = lora_pad(a1, a2)
    g1p, g2p = lora_pad(g1, g2)
    t_w = matmul(x_w, w1p, act="tanh", out_dtype=BF16)
    t_a = matmul(x_a, a1p, out_dtype=BF16)
    t_g = matmul(x_g, g1p, act="sigmoid", out_dtype=BF16)

    def lora_specs(t, w):
        return (pl.BlockSpec((tm, t.shape[1]), lambda i: (i, 0)), pl.BlockSpec(w.shape, lambda i: (0, 0)))

    log_w, k_mod, a_neg, b_vec = pl.pallas_call(
        _rwkv_prep_kernel,
        grid=(m // tm,),
        in_specs=[row_spec, lora_specs(t_w, w2p)[0], lora_specs(t_a, a2p)[0], lora_specs(t_w, w2p)[1],
                  lora_specs(t_a, a2p)[1]] + [vec_spec] * 4,
        out_specs=[row_spec] * 4,
        out_shape=[jax.ShapeDtypeStruct((m, d), F32)] + [jax.ShapeDtypeStruct((m, d), BF16)] * 3,
        compiler_params=_params("parallel"),
        name="rwkv_prep",
    )(k, t_w, t_a, w2p.astype(BF16), a2p.astype(BF16), w0.reshape(1, d), a0.reshape(1, d), k_k.reshape(1, d),
      k_a.reshape(1, d))

    tb, groups_per_call = 256, 8
    width = groups_per_call * RWKV_GROUP * RWKV_HEAD_SIZE
    nt = seq // tb
    blk = pl.BlockSpec((tb, width), lambda b, p, t: (b * nt + t, p))
    y = pl.pallas_call(
        _rwkv_chunk_kernel,
        grid=(batch, d // width, nt),
        in_specs=[blk] * 6,
        out_specs=blk,
        out_shape=jax.ShapeDtypeStruct((m, d), F32),
        scratch_shapes=[pltpu.VMEM((groups_per_call, RWKV_GROUP * RWKV_HEAD_SIZE, RWKV_GROUP * RWKV_HEAD_SIZE), F32)],
        compiler_params=_params("parallel", "parallel", "arbitrary"),
        name="rwkv_chunks",
    )(r, log_w, k_mod, v, a_neg, b_vec)

    out = pl.pallas_call(
        _rwkv_post_kernel,
        grid=(m // tm,),
        in_specs=[row_spec] * 4 + list(lora_specs(t_g, g2p)) + [vec_spec] * 3,
        out_specs=row_spec,
        out_shape=jax.ShapeDtypeStruct((m, d), BF16),
        compiler_params=_params("parallel"),
        name="rwkv_post",
    )(y, r, k_mod, v, t_g, g2p.astype(BF16), r_k.reshape(1, d), lnx_w.reshape(1, d), lnx_b.reshape(1, d))
    return matmul(out, w_o, residual=x, norm_gain=next_norm, tm=512)


def _softcap(z):
    return GATE_SOFTCAP * jnp.tanh(z / GATE_SOFTCAP)


def _mlstm_kernel(q_ref, k_ref, v_ref, o_ref, gc_ref, gr_ref, hg_ref, out_ref, ct_scr, n_scr, m_scr):
    L, H, dk, dv = MLSTM_CHUNK, MLSTM_HEADS, MLSTM_DQK, MLSTM_DV
    heads = range(H)

    @pl.when(pl.program_id(1) == 0)
    def _():
        ct_scr[...] = jnp.zeros_like(ct_scr)
        n_scr[...] = jnp.zeros_like(n_scr)
        m_scr[...] = jnp.zeros_like(m_scr)

    t_idx = lax.broadcasted_iota(jnp.int32, (L, L), 0)
    s_idx = lax.broadcasted_iota(jnp.int32, (L, L), 1)
    causal = s_idx <= t_idx
    anti = t_idx <= s_idx
    nt_dims = (((1,), (1,)), ((), ()))

    def chunk(c, carry):
        sl = pl.ds(pl.multiple_of(c * L, L), L)
        gc, gr = gc_ref[sl, :], gr_ref[c]
        li_cols, li_rows = _softcap(gc[:, :H]), _softcap(gr[:H, :])
        lf_cols = jax.nn.log_sigmoid(_softcap(gc[:, H:]))
        lf_rows = jax.nn.log_sigmoid(_softcap(gr[H:, :]))
        li_col = [li_cols[:, h:h + 1] for h in heads]
        li_row = [li_rows[h:h + 1, :] for h in heads]
        lf_col = [lf_cols[:, h:h + 1] for h in heads]
        lf_row = [lf_rows[h:h + 1, :] for h in heads]
        b_col = [jnp.sum(jnp.where(causal, lf_row[h], 0.0), axis=1, keepdims=True) for h in heads]
        b_row = [jnp.sum(jnp.where(anti, lf_col[h], 0.0), axis=0, keepdims=True) for h in heads]
        b_last = [jnp.sum(lf_row[h], axis=1, keepdims=True) for h in heads]
        m_prev = [m_scr[h] for h in heads]
        dmat = [jnp.where(causal, b_col[h] - b_row[h] + li_row[h], NEG_BIG) for h in heads]
        inter = [b_col[h] + m_prev[h] for h in heads]
        m_t = [jnp.maximum(inter[h], jnp.max(dmat[h], axis=1, keepdims=True)) for h in heads]
        q = [q_ref[sl, h * dk:(h + 1) * dk].astype(F32) for h in heads]
        k = [k_ref[sl, h * dk:(h + 1) * dk].astype(F32) * (dk ** -0.5) for h in heads]
        vb = [v_ref[sl, h * dv:(h + 1) * dv].astype(BF16) for h in heads]
        qb = [x.astype(BF16) for x in q]
        s = [lax.dot_general(qb[h], k[h].astype(BF16), nt_dims, preferred_element_type=F32) * jnp.exp(dmat[h] - m_t[h])
             for h in heads]
        w_inter = [jnp.exp(inter[h] - m_t[h]) for h in heads]
        ct = [ct_scr[h] for h in heads]
        n_row = [n_scr[h] for h in heads]
        num = [jnp.dot(s[h].astype(BF16), vb[h], preferred_element_type=F32)
               + w_inter[h] * jnp.dot(qb[h], ct[h].astype(BF16), preferred_element_type=F32) for h in heads]
        den = [jnp.sum(s[h], axis=1, keepdims=True) + w_inter[h] * jnp.sum(q[h] * n_row[h], axis=1, keepdims=True)
               for h in heads]
        h_c = [num[h] / jnp.maximum(jnp.abs(den[h]), jnp.exp(-m_t[h])) for h in heads]
        for h in heads:
            hn = h_c[h] * lax.rsqrt(jnp.mean(h_c[h] * h_c[h], axis=-1, keepdims=True) + NORM_EPS)
            cols = slice(h * dv, (h + 1) * dv)
            out_ref[sl, cols] = (hn * hg_ref[:, cols] * jax.nn.sigmoid(o_ref[sl, cols].astype(F32))).astype(out_ref.dtype)
        d_row = [b_last[h] - b_row[h] + li_row[h] for h in heads]
        m_new = [jnp.maximum(b_last[h] + m_prev[h], jnp.max(d_row[h], axis=1, keepdims=True)) for h in heads]
        kw = [k[h] * jnp.exp(b_last[h] - b_col[h] + li_col[h] - m_new[h]) for h in heads]
        w_c = [jnp.exp(b_last[h] + m_prev[h] - m_new[h]) for h in heads]
        for h in heads:
            ct_scr[h] = w_c[h] * ct[h] + jnp.dot(kw[h].T.astype(BF16), vb[h], preferred_element_type=F32)
            n_scr[h] = w_c[h] * n_row[h] + jnp.sum(kw[h], axis=0, keepdims=True)
            m_scr[h] = m_new[h]
        return carry

    lax.fori_loop(0, q_ref.shape[0] // L, chunk, 0)


def mlstm_layer(x, norm_g, w_in, b_if, head_gain, w_out, next_norm, batch, seq):
    m, d = x.shape
    H, L, dk, dv = MLSTM_HEADS, MLSTM_CHUNK, MLSTM_DQK, MLSTM_DV
    tb = 4 * L
    nt = seq // tb
    h = rmsnorm(x, norm_g)
    n_main = 2 * H * dk + 2 * H * dv
    proj = matmul(h, w_in, n=n_main, out_dtype=BF16)
    gates = matmul(h, _pad_cols(w_in[:, n_main:], LANES), bias=_pad_cols(b_if.reshape(1, 2 * H), LANES))
    gates = gates[:, :2 * H]
    gates_t = jnp.transpose(gates.reshape(m // L, L, 2 * H), (0, 2, 1))
    out = pl.pallas_call(
        _mlstm_kernel,
        grid=(batch, nt),
        in_specs=[pl.BlockSpec((tb, H * dk), lambda b, t: (b * nt + t, 0)),
                  pl.BlockSpec((tb, H * dk), lambda b, t: (b * nt + t, 1)),
                  pl.BlockSpec((tb, H * dv), lambda b, t: (b * nt + t, 1)),
                  pl.BlockSpec((tb, H * dv), lambda b, t: (b * nt + t, 2)),
                  pl.BlockSpec((tb, 2 * H), lambda b, t: (b * nt + t, 0)),
                  pl.BlockSpec((tb // L, 2 * H, L), lambda b, t: (b * nt + t, 0, 0)),
                  pl.BlockSpec((1, H * dv), lambda b, t: (0, 0))],
        out_specs=pl.BlockSpec((tb, H * dv), lambda b, t: (b * nt + t, 0)),
        out_shape=jax.ShapeDtypeStruct((m, H * dv), BF16),
        scratch_shapes=[pltpu.VMEM((H, dk, dv), F32), pltpu.VMEM((H, 1, dk), F32), pltpu.VMEM((H, 1, 1), F32)],
        compiler_params=_params("parallel", "arbitrary"),
        name="mlstm_chunks",
    )(proj, proj, proj, proj, gates, gates_t, head_gain.reshape(1, -1))
    return matmul(out, w_out, residual=x, norm_gain=next_norm, tm=512)


def kernel(x, moba_norm, moba_wqkv, moba_q_gain, moba_k_gain, moba_wo, rwkv_norm, rwkv_mu, rwkv_w_r, rwkv_w_k, rwkv_w_v, rwkv_w_o, rwkv_w0, rwkv_w1, rwkv_w2, rwkv_a0, rwkv_a1, rwkv_a2, rwkv_g1, rwkv_g2, rwkv_k_k, rwkv_k_a, rwkv_r_k, rwkv_lnx_w, rwkv_lnx_b, mlstm_norm, mlstm_w_in, mlstm_b_if, mlstm_head_gain, mlstm_w_out, ffn_norm, ffn_w_up, ffn_conv_w, ffn_conv_b, ffn_w_down):
    batch, seq, d = x.shape
    depth = ffn_norm.shape[0]
    x = x.reshape(batch * seq, d)
    for i in range(depth):
        kind, j = i % 3, i // 3
        if kind == 0:
            x, h = moba_layer(x, moba_norm[j], moba_wqkv, moba_q_gain[j], moba_k_gain[j], moba_wo, j,
                              ffn_norm[i], batch, seq)
        elif kind == 1:
            x, h = rwkv_layer(x, rwkv_norm[j], rwkv_mu[j], rwkv_w_r[j], rwkv_w_k[j], rwkv_w_v[j], rwkv_w_o[j],
                           rwkv_w0[j], rwkv_w1[j], rwkv_w2[j], rwkv_a0[j], rwkv_a1[j], rwkv_a2[j],
                           rwkv_g1[j], rwkv_g2[j], rwkv_k_k[j], rwkv_k_a[j], rwkv_r_k[j],
                           rwkv_lnx_w[j], rwkv_lnx_b[j], ffn_norm[i], batch, seq)
        else:
            x, h = mlstm_layer(x, mlstm_norm[j], mlstm_w_in[j], mlstm_b_if[j], mlstm_head_gain[j],
                               mlstm_w_out[j], ffn_norm[i], batch, seq)
        x = conv_ffn(x, h, ffn_w_up, ffn_conv_w, ffn_conv_b, ffn_w_down, i, seq)
    return x.reshape(batch, seq, d)
```

```python
import functools

import jax
import jax.numpy as jnp
from jax import lax
from jax.experimental import pallas as pl
from jax.experimental.pallas import tpu as pltpu

F32 = jnp.float32
BF16 = jnp.bfloat16

LANES = 128
SUBLANES = 8
VMEM_LIMIT_BYTES = 56 * 1024 * 1024

NORM_EPS = 1e-6
MOBA_HEADS = 16
MOBA_HEAD_DIM = 128
MOBA_BLOCK = 256
MOBA_TOPK = 3
MOBA_HEADS_PER_CALL = 4
MOBA_VROWS = MOBA_HEAD_DIM + 16
MOBA_Q_SCALE = 1.4426950408889634 * MOBA_HEAD_DIM ** -0.5
ROPE_THETA = 500000.0
ROT_DIM = MOBA_HEAD_DIM // 4
RWKV_HEAD_SIZE = 64
RWKV_GN_EPS = 64e-5
RWKV_DECAY_SCALE = 0.6065306597126334
RWKV_CHUNK = 64
RWKV_GROUP = 4
MLSTM_HEADS = 8
MLSTM_DV = 256
MLSTM_DQK = 128
MLSTM_CHUNK = 64
GATE_SOFTCAP = 15.0
CONV_WIDTH = 3
FFN_HALO = 16
NEG_BIG = -1e30

HI = lax.Precision.HIGHEST


def _params(*semantics):
    return pltpu.CompilerParams(dimension_semantics=semantics, vmem_limit_bytes=VMEM_LIMIT_BYTES)


def _pad_cols(w, n):
    return jnp.pad(w, ((0, 0), (0, n - w.shape[1])))


def _pad_rows(w, n):
    return jnp.pad(w, ((0, n - w.shape[0]), (0, 0)))


def _rmsnorm_kernel(x_ref, g_ref, o_ref):
    x = x_ref[...]
    y = x * lax.rsqrt(jnp.mean(x * x, axis=-1, keepdims=True) + NORM_EPS)
    o_ref[...] = (y * g_ref[...]).astype(o_ref.dtype)


def rmsnorm(x, gain, tm=512):
    m, d = x.shape
    return pl.pallas_call(
        _rmsnorm_kernel,
        grid=(m // tm,),
        in_specs=[pl.BlockSpec((tm, d), lambda i: (i, 0)), pl.BlockSpec((1, d), lambda i: (0, 0))],
        out_specs=pl.BlockSpec((tm, d), lambda i: (i, 0)),
        out_shape=jax.ShapeDtypeStruct((m, d), BF16),
        compiler_params=_params("parallel"),
        name="rmsnorm",
    )(x, gain.reshape(1, d))


def _mm_kernel(*refs, act, has_bias, has_res, has_norm, cast_w):
    x_ref, w_ref = refs[0], refs[1]
    if cast_w:
        w_ref, refs = refs[-1], refs[:-1]

        @pl.when(pl.program_id(1) == 0)
        def _():
            w_ref[...] = refs[1][...].astype(BF16)

    acc = jnp.dot(x_ref[...], w_ref[...], preferred_element_type=F32)
    k = 2
    if has_bias:
        acc = acc + refs[k][...]
        k += 1
    if act == "tanh":
        acc = jnp.tanh(acc)
    elif act == "sigmoid":
        acc = jax.nn.sigmoid(acc)
    if has_res:
        acc = acc + refs[k][...]
        k += 1
    if has_norm:
        o_ref, n_ref = refs[-2], refs[-1]
        o_ref[...] = acc.astype(o_ref.dtype)
        y = acc * lax.rsqrt(jnp.mean(acc * acc, axis=-1, keepdims=True) + NORM_EPS)
        n_ref[...] = (y * refs[k][...]).astype(n_ref.dtype)
    else:
        refs[-1][...] = acc.astype(refs[-1].dtype)


def matmul(x, w, *, layer=None, n=None, bias=None, act=None, residual=None, norm_gain=None, out_dtype=F32,
           tm=1024, tn=1024):
    m, kdim = x.shape
    n = w.shape[-1] if n is None else n
    tm, tn = min(tm, m), min(tn, n)
    if norm_gain is not None:
        tn = n
    assert m % tm == 0 and n % tn == 0, (m, n, tm, tn)
    cast_w = w.dtype != BF16
    if layer is None:
        w_spec = pl.BlockSpec((kdim, tn), lambda j, i: (0, j))
    else:
        w_spec = pl.BlockSpec((None, kdim, tn), lambda j, i: (layer, 0, j))
    in_specs = [pl.BlockSpec((tm, kdim), lambda j, i: (i, 0)), w_spec]
    args = [x, w]
    if bias is not None:
        in_specs.append(pl.BlockSpec((1, tn), lambda j, i: (0, j)))
        args.append(bias.reshape(1, n))
    if residual is not None:
        in_specs.append(pl.BlockSpec((tm, tn), lambda j, i: (i, j)))
        args.append(residual)
    out_spec = pl.BlockSpec((tm, tn), lambda j, i: (i, j))
    out_specs, out_shape = out_spec, jax.ShapeDtypeStruct((m, n), out_dtype)
    if norm_gain is not None:
        in_specs.append(pl.BlockSpec((1, tn), lambda j, i: (0, j)))
        args.append(norm_gain.reshape(1, n))
        out_specs, out_shape = [out_spec, out_spec], [out_shape, jax.ShapeDtypeStruct((m, n), BF16)]
    return pl.pallas_call(
        functools.partial(_mm_kernel, act=act, has_bias=bias is not None, has_res=residual is not None,
                          has_norm=norm_gain is not None, cast_w=cast_w),
        grid=(n // tn, m // tm),
        in_specs=in_specs,
        out_specs=out_specs,
        out_shape=out_shape,
        scratch_shapes=[pltpu.VMEM((kdim, tn), BF16)] if cast_w else [],
        compiler_params=_params("parallel", "arbitrary"),
        name="matmul",
    )(*args)


def _ffn_up_kernel(x_ref, h_ref, wg_ref, wu_ref, cwg_ref, cwu_ref, cbg_ref, cbu_ref, o_ref, wgb_scr, wub_scr,
                   *, tiles_per_seq):
    i = pl.program_id(1)
    first = (i % tiles_per_seq) == 0

    @pl.when(i == 0)
    def _():
        wgb_scr[...] = wg_ref[...].astype(BF16)
        wub_scr[...] = wu_ref[...].astype(BF16)

    halo = jnp.where(first, jnp.zeros_like(h_ref[...]), h_ref[...])
    x = jnp.concatenate([halo, x_ref[...]], axis=0)

    def branch(w_scr, cw_ref, cb_ref):
        u = jnp.dot(x, w_scr[...], preferred_element_type=F32)
        cw = cw_ref[...]
        return pltpu.roll(u, 2, 0) * cw[0:1, :] + pltpu.roll(u, 1, 0) * cw[1:2, :] + u * cw[2:3, :] + cb_ref[...]

    gate = branch(wgb_scr, cwg_ref, cbg_ref)
    act = gate * jax.nn.sigmoid(gate) * branch(wub_scr, cwu_ref, cbu_ref)
    o_ref[...] = act[FFN_HALO:].astype(o_ref.dtype)


def ffn_up(h, w_up, conv_w, conv_b, layer, seq, tm=1024, tn=512):
    m, d = h.shape
    f = w_up.shape[-1] // 2
    nf = f // tn
    assert f % tn == 0 and seq % tm == 0
    hb = tm // FFN_HALO
    return pl.pallas_call(
        functools.partial(_ffn_up_kernel, tiles_per_seq=seq // tm),
        grid=(nf, m // tm),
        in_specs=[
            pl.BlockSpec((tm, d), lambda c, i: (i, 0)),
            pl.BlockSpec((FFN_HALO, d), lambda c, i: (jnp.maximum(i * hb - 1, 0), 0)),
            pl.BlockSpec((None, d, tn), lambda c, i: (layer, 0, c)),
            pl.BlockSpec((None, d, tn), lambda c, i: (layer, 0, c + nf)),
            pl.BlockSpec((None, CONV_WIDTH, tn), lambda c, i: (layer, 0, c)),
            pl.BlockSpec((None, CONV_WIDTH, tn), lambda c, i: (layer, 0, c + nf)),
            pl.BlockSpec((None, 1, tn), lambda c, i: (layer, 0, c)),
            pl.BlockSpec((None, 1, tn), lambda c, i: (layer, 0, c + nf)),
        ],
        out_specs=pl.BlockSpec((tm, tn), lambda c, i: (i, c)),
        out_shape=jax.ShapeDtypeStruct((m, f), BF16),
        scratch_shapes=[pltpu.VMEM((d, tn), BF16), pltpu.VMEM((d, tn), BF16)],
        compiler_params=_params("parallel", "arbitrary"),
        name="ffn_up_conv",
    )(h, h, w_up, w_up, conv_w, conv_w, conv_b[:, None, :], conv_b[:, None, :])


def conv_ffn(x, h, w_up, conv_w, conv_b, w_down, layer, seq):
    act = ffn_up(h, w_up, conv_w, conv_b, layer, seq)
    return matmul(act, w_down, layer=layer, residual=x, tm=512, tn=512)


def _rope_tables(seq):
    half = ROT_DIM // 2
    inv_freq = jnp.float32(ROPE_THETA) ** (-jnp.arange(0, ROT_DIM, 2, dtype=F32) / ROT_DIM)
    ang = jnp.arange(seq, dtype=F32)[:, None] * inv_freq[None, :]
    cos, sin = jnp.cos(ang), jnp.sin(ang)
    ones = jnp.ones((seq, MOBA_HEAD_DIM - ROT_DIM), F32)
    zeros_r = jnp.zeros((seq, MOBA_HEAD_DIM - ROT_DIM), F32)
    c_tab = jnp.concatenate([cos, cos, ones], axis=1)
    s_tab = jnp.concatenate([sin, sin, zeros_r], axis=1)
    return c_tab, s_tab


def _moba_prep_kernel(qkv_ref, c_ref, s_ref, qg_ref, kg_ref,
                      q_ref, k_ref, vt_ref, sel_ref, kmean_scr, *, blocks_per_seq):
    n = pl.program_id(0) % blocks_per_seq
    d = MOBA_HEADS * MOBA_HEAD_DIM
    half = ROT_DIM // 2
    c_tab, s_tab = c_ref[...], s_ref[...]
    mean_sq = jnp.full((MOBA_HEAD_DIM, MOBA_HEAD_DIM), 1.0 / MOBA_HEAD_DIM, BF16)
    src = lax.broadcasted_iota(jnp.int32, (MOBA_HEAD_DIM, MOBA_HEAD_DIM), 0)
    dst = lax.broadcasted_iota(jnp.int32, (MOBA_HEAD_DIM, MOBA_HEAD_DIM), 1)
    rot = jnp.where((dst < half) & (src == dst + half), -1.0,
                    jnp.where((dst >= half) & (dst < 2 * half) & (src == dst - half), 1.0, 0.0)).astype(BF16)

    @pl.when(n == 0)
    def _():
        kmean_scr[...] = jnp.zeros_like(kmean_scr)

    def norm_rope(x, gain):
        y = x * lax.rsqrt(jnp.dot((x * x).astype(BF16), mean_sq, preferred_element_type=F32) + NORM_EPS) * gain
        return y * c_tab + _split_dot(y, rot, 2) * s_tab

    nb = sel_ref.shape[1]
    blk = lax.broadcasted_iota(jnp.int32, (nb, MOBA_BLOCK), 0)
    eligible = blk < n
    for h in range(MOBA_HEADS):
        lo = h * MOBA_HEAD_DIM
        q = norm_rope(qkv_ref[:, lo:lo + MOBA_HEAD_DIM].astype(F32), qg_ref[...])
        k = norm_rope(qkv_ref[:, d + lo:d + lo + MOBA_HEAD_DIM].astype(F32), kg_ref[...])
        q_ref[:, lo:lo + MOBA_HEAD_DIM] = (q * MOBA_Q_SCALE).astype(q_ref.dtype)
        k_ref[:, lo:lo + MOBA_HEAD_DIM] = k.astype(k_ref.dtype)
        vlo = h * MOBA_VROWS
        vt_ref[vlo:vlo + MOBA_HEAD_DIM, :] = qkv_ref[:, 2 * d + lo:2 * d + lo + MOBA_HEAD_DIM].astype(F32).T.astype(vt_ref.dtype)
        vt_ref[vlo + MOBA_HEAD_DIM:vlo + MOBA_VROWS, :] = jnp.ones((MOBA_VROWS - MOBA_HEAD_DIM, MOBA_BLOCK), vt_ref.dtype)
        gate = lax.dot_general(kmean_scr[:, lo:lo + MOBA_HEAD_DIM], q, (((1,), (1,)), ((), ())),
                               precision=HI, preferred_element_type=F32)
        gate = jnp.where(eligible, gate, -jnp.inf)
        sel = jnp.zeros((nb, MOBA_BLOCK), F32)
        for j in range(nb):
            gj = gate[j:j + 1, :]
            beats = jnp.where(gate > gj, 1.0, jnp.where((gate == gj) & (blk < j), 1.0, 0.0))
            rank = jnp.sum(beats, axis=0, keepdims=True)
            chosen = jnp.where((rank < MOBA_TOPK) & (j < n), 1.0, 0.0)
            sel = jnp.where(blk == j, chosen, sel)
        sel_ref[h] = (sel - 1.0) * -NEG_BIG
        km_rows = lax.broadcasted_iota(jnp.int32, (nb, MOBA_HEAD_DIM), 0)
        kmean_scr[:, lo:lo + MOBA_HEAD_DIM] = jnp.where(km_rows == n, jnp.mean(k, axis=0, keepdims=True),
                                                        kmean_scr[:, lo:lo + MOBA_HEAD_DIM])


def _moba_attn_kernel(q_ref, k_ref, vt_ref, bias_ref, o_ref, s_scr):
    i = pl.program_id(2)
    heads = range(MOBA_HEADS_PER_CALL)
    lanes = [slice(h * MOBA_HEAD_DIM, (h + 1) * MOBA_HEAD_DIM) for h in heads]
    vrows = [slice(h * MOBA_VROWS, (h + 1) * MOBA_VROWS) for h in heads]
    q = [q_ref[:, ln] for ln in lanes]
    nt_dims = (((1,), (1,)), ((), ()))
    pair = 2 * MOBA_BLOCK
    n_pairs = (i + 2) // 2
    kpos = lax.broadcasted_iota(jnp.int32, (MOBA_BLOCK, MOBA_BLOCK), 0)
    qpos = lax.broadcasted_iota(jnp.int32, (MOBA_BLOCK, MOBA_BLOCK), 1)
    causal_bias = jnp.where(kpos <= qpos, 0.0, NEG_BIG)

    def store_scores(p, m_run):
        start = pl.multiple_of(p * pair, pair)
        m_new = []
        for h in heads:
            s = lax.dot_general(k_ref[pl.ds(start, pair), lanes[h]], q[h], nt_dims, preferred_element_type=F32)
            bias = jnp.concatenate([jnp.where(2 * p + e == i, causal_bias, bias_ref[h, pl.ds(2 * p + e, 1), :])
                                    for e in range(2)], axis=0)
            s = s + bias
            s_scr[h, pl.ds(start, pair), :] = s
            m_new.append(jnp.maximum(m_run[h], jnp.max(s.reshape(pair // SUBLANES, SUBLANES, MOBA_BLOCK), axis=0)))
        return tuple(m_new)

    m_part = lax.fori_loop(0, n_pairs, store_scores,
                           tuple(jnp.full((SUBLANES, MOBA_BLOCK), NEG_BIG, F32) for _ in heads))
    m_fin = [jnp.max(x, axis=0, keepdims=True) for x in m_part]

    def accumulate(p, acc):
        start = pl.multiple_of(p * pair, pair)
        return tuple(
            acc[h] + jnp.dot(vt_ref[vrows[h], pl.ds(start, pair)],
                             jnp.exp2(s_scr[h, pl.ds(start, pair), :] - m_fin[h]).astype(BF16),
                             preferred_element_type=F32)
            for h in heads)

    acc = lax.fori_loop(0, n_pairs, accumulate,
                        tuple(jnp.zeros((MOBA_VROWS, MOBA_BLOCK), F32) for _ in heads))
    for h in heads:
        out_t = acc[h][:MOBA_HEAD_DIM] / acc[h][MOBA_HEAD_DIM:MOBA_HEAD_DIM + 1]
        o_ref[:, lanes[h]] = out_t.T.astype(o_ref.dtype)


def moba_layer(x, norm_g, wqkv, q_gain, k_gain, wo, layer, next_norm, batch, seq):
    m, d = x.shape
    nb = seq // MOBA_BLOCK
    h = rmsnorm(x, norm_g)
    qkv = matmul(h, wqkv, layer=layer, out_dtype=BF16)
    c_tab, s_tab = _rope_tables(seq)
    tab_spec = pl.BlockSpec((MOBA_BLOCK, MOBA_HEAD_DIM), lambda i: (i % nb, 0))
    gain_spec = pl.BlockSpec((1, MOBA_HEAD_DIM), lambda i: (0, 0))
    q, k, vt, sel = pl.pallas_call(
        functools.partial(_moba_prep_kernel, blocks_per_seq=nb),
        grid=(m // MOBA_BLOCK,),
        in_specs=[pl.BlockSpec((MOBA_BLOCK, 3 * d), lambda i: (i, 0)), tab_spec, tab_spec, gain_spec, gain_spec],
        out_specs=[pl.BlockSpec((MOBA_BLOCK, d), lambda i: (i, 0)),
                   pl.BlockSpec((MOBA_BLOCK, d), lambda i: (i, 0)),
                   pl.BlockSpec((MOBA_HEADS * MOBA_VROWS, MOBA_BLOCK), lambda i: (0, i)),
                   pl.BlockSpec((None, MOBA_HEADS, nb, MOBA_BLOCK), lambda i: (i, 0, 0, 0))],
        out_shape=[jax.ShapeDtypeStruct((m, d), BF16), jax.ShapeDtypeStruct((m, d), BF16),
                   jax.ShapeDtypeStruct((MOBA_HEADS * MOBA_VROWS, m), BF16),
                   jax.ShapeDtypeStruct((m // MOBA_BLOCK, MOBA_HEADS, nb, MOBA_BLOCK), F32)],
        scratch_shapes=[pltpu.VMEM((nb, d), F32)],
        compiler_params=_params("arbitrary"),
        name="moba_prep",
    )(qkv, c_tab, s_tab, q_gain.reshape(1, -1), k_gain.reshape(1, -1))
    hw = MOBA_HEADS_PER_CALL * MOBA_HEAD_DIM
    attn = pl.pallas_call(
        _moba_attn_kernel,
        grid=(batch, MOBA_HEADS // MOBA_HEADS_PER_CALL, nb),
        in_specs=[pl.BlockSpec((MOBA_BLOCK, hw), lambda b, hh, i: (b * nb + i, hh)),
                  pl.BlockSpec((seq, hw), lambda b, hh, i: (b, hh)),
                  pl.BlockSpec((MOBA_HEADS_PER_CALL * MOBA_VROWS, seq), lambda b, hh, i: (hh, b)),
                  pl.BlockSpec((None, MOBA_HEADS_PER_CALL, nb, MOBA_BLOCK), lambda b, hh, i: (b * nb + i, hh, 0, 0))],
        out_specs=pl.BlockSpec((MOBA_BLOCK, hw), lambda b, hh, i: (b * nb + i, hh)),
        out_shape=jax.ShapeDtypeStruct((m, d), BF16),
        scratch_shapes=[pltpu.VMEM((MOBA_HEADS_PER_CALL, seq, MOBA_BLOCK), F32)],
        compiler_params=_params("parallel", "parallel", "parallel"),
        name="moba_attn",
    )(q, k, vt, sel)
    return matmul(attn, wo, layer=layer, residual=x, norm_gain=next_norm, tm=512)


def _head_sum_matrix():
    r = lax.broadcasted_iota(jnp.int32, (LANES, LANES), 0) // RWKV_HEAD_SIZE
    c = lax.broadcasted_iota(jnp.int32, (LANES, LANES), 1) // RWKV_HEAD_SIZE
    return jnp.where(r == c, 1.0, 0.0).astype(BF16)


def _split_dot(x, y_exact, pieces):
    out = None
    for _ in range(pieces):
        part = x.astype(BF16)
        term = jnp.dot(part, y_exact, preferred_element_type=F32)
        out = term if out is None else out + term
        x = x - part.astype(F32)
    return out


def _head_sum(x, ones_bd):
    return _split_dot(x, ones_bd, 2)


def _rwkv_mix_kernel(x_ref, h_ref, g_ref, mu_ref, w1_ref, a1_ref, g1_ref,
                     xr_ref, xk_ref, xv_ref, tw_ref, ta_ref, tg_ref, *, tiles_per_seq):
    first = (pl.program_id(0) % tiles_per_seq) == 0

    def norm(x):
        return (x * lax.rsqrt(jnp.mean(x * x, axis=-1, keepdims=True) + NORM_EPS) * g_ref[...])

    h = norm(x_ref[...])
    prev_row = jnp.where(first, 0.0, norm(h_ref[...])[SUBLANES - 1:SUBLANES, :])
    row = lax.broadcasted_iota(jnp.int32, h.shape, 0)
    xx = jnp.where(row == 0, prev_row, pltpu.roll(h, 1, 0)) - h
    def mix(idx):
        return (h + xx * mu_ref[idx:idx + 1, :]).astype(BF16)

    xr_ref[...] = mix(0)
    xk_ref[...] = mix(2)
    xv_ref[...] = mix(3)
    tw_ref[...] = jnp.tanh(jnp.dot(mix(1), w1_ref[...], preferred_element_type=F32)).astype(tw_ref.dtype)
    ta_ref[...] = jnp.dot(mix(4), a1_ref[...], preferred_element_type=F32).astype(ta_ref.dtype)
    tg_ref[...] = jax.nn.sigmoid(jnp.dot(mix(5), g1_ref[...], preferred_element_type=F32)).astype(tg_ref.dtype)


def _rwkv_prep_kernel(k_ref, tw_ref, ta_ref, w2_ref, a2_ref, w0_ref, a0_ref, kk_ref, ka_ref,
                      logw_ref, kmod_ref, an_ref, b_ref):
    ones_bd = _head_sum_matrix()
    w_pre = jnp.dot(tw_ref[...], w2_ref[...], preferred_element_type=F32)
    a_pre = jnp.dot(ta_ref[...], a2_ref[...], preferred_element_type=F32)
    logw_ref[...] = -RWKV_DECAY_SCALE * jax.nn.sigmoid(w0_ref[...] + w_pre)
    a = jax.nn.sigmoid(a0_ref[...] + a_pre)
    k = k_ref[...].astype(F32)
    kmod_ref[...] = (k * (1.0 + (a - 1.0) * ka_ref[...])).astype(kmod_ref.dtype)
    kk = k * kk_ref[...]
    for c in range(kk.shape[1] // LANES):
        sl = slice(c * LANES, (c + 1) * LANES)
        kc = kk[:, sl]
        nrm = jnp.maximum(jnp.sqrt(_head_sum(kc * kc, ones_bd)), 1e-12)
        kc = kc / nrm
        an_ref[:, sl] = (-kc).astype(an_ref.dtype)
        b_ref[:, sl] = (kc * a[:, sl]).astype(b_ref.dtype)


def _rwkv_chunk_kernel(r_ref, lw_ref, k_ref, v_ref, a_ref, b_ref, y_ref, s_scr):
    L, n, grp = RWKV_CHUNK, RWKV_HEAD_SIZE, RWKV_GROUP
    width = grp * n
    rows = grp * L
    tb = r_ref.shape[0]

    @pl.when(pl.program_id(2) == 0)
    def _():
        s_scr[...] = jnp.zeros_like(s_scr)

    row = lax.broadcasted_iota(jnp.int32, (rows, rows), 0)
    col = lax.broadcasted_iota(jnp.int32, (rows, rows), 1)
    strict = (col % L) < (row % L)
    incl = (col % L) <= (row % L)
    lane_head = lax.broadcasted_iota(jnp.int32, (L, width), 1) // n
    cum = jnp.where(lax.broadcasted_iota(jnp.int32, (L, L), 1) <= lax.broadcasted_iota(jnp.int32, (L, L), 0),
                    1.0, 0.0).astype(BF16)

    def cumsum_rows(x):
        out = None
        for _ in range(3):
            part = x.astype(BF16)
            term = jnp.dot(cum, part, preferred_element_type=F32)
            out = term if out is None else out + term
            x = x - part.astype(F32)
        return out
    nt_dims = (((1,), (1,)), ((), ()))
    tn_dims = (((0,), (0,)), ((), ()))

    def stack(x):
        return jnp.concatenate([jnp.where(lane_head == h, x, 0.0) for h in range(grp)], axis=0)

    def unstack(xm):
        out = xm[0:L]
        for h in range(1, grp):
            out = out + xm[h * L:(h + 1) * L]
        return out

    def mm(x, y):
        return jnp.dot(x.astype(BF16), y.astype(BF16), preferred_element_type=F32)

    n_groups = r_ref.shape[1] // width
    groups = range(n_groups)

    def chunk(c, carry):
        sl = pl.ds(pl.multiple_of(c * L, L), L)
        lanes = [slice(q * width, (q + 1) * width) for q in groups]
        lw = [lw_ref[sl, ln] for ln in lanes]
        g = [cumsum_rows(x) for x in lw]
        g_last = [x[L - 1:L, :] for x in g]
        e_neg = [jnp.exp(-x) for x in g]
        r, k, v, a, b = ([ref[sl, ln].astype(F32) for ln in lanes] for ref in (r_ref, k_ref, v_ref, a_ref, b_ref))
        lhs = [jnp.concatenate([stack(a[q] * jnp.exp(g[q] - lw[q])), stack(r[q] * jnp.exp(g[q]))], axis=0).astype(BF16)
               for q in groups]
        rhs = [jnp.concatenate([stack(b[q] * e_neg[q]), stack(k[q] * e_neg[q])], axis=0).astype(BF16) for q in groups]
        v_m = [stack(x) for x in v]
        s = [s_scr[q] for q in groups]
        prod = [lax.dot_general(lhs[q], rhs[q], nt_dims, preferred_element_type=F32) for q in groups]
        from_state = [lax.dot_general(lhs[q], s[q].astype(BF16), nt_dims, preferred_element_type=F32) for q in groups]
        a_ak = [jnp.where(strict, p[:rows, rows:], 0.0) for p in prod]
        u_m = [from_state[q][:rows] + mm(a_ak[q], v_m[q]) for q in groups]
        power = [jnp.where(strict, p[:rows, :rows], 0.0) for p in prod]
        for level in range(L.bit_length() - 1):
            if level:
                power = [mm(p, p) for p in power]
            u_m = [u_m[q] + mm(power[q], u_m[q]) for q in groups]
        for q in groups:
            m_both = jnp.where(jnp.concatenate([incl, incl], axis=1), prod[q][rows:, :], 0.0).astype(BF16)
            uv = jnp.concatenate([u_m[q], v_m[q]], axis=0).astype(BF16)
            y_ref[sl, lanes[q]] = unstack(from_state[q][rows:] + jnp.dot(m_both, uv, preferred_element_type=F32))
            e_tail = jnp.exp(g_last[q] - g[q])
            tail = jnp.concatenate([stack(b[q] * e_tail), stack(k[q] * e_tail)], axis=0).astype(BF16)
            s_scr[q] = s[q] * jnp.exp(g_last[q]) + lax.dot_general(uv, tail, tn_dims, preferred_element_type=F32)
        return carry

    lax.fori_loop(0, tb // L, chunk, 0)


def _rwkv_post_kernel(y_ref, r_ref, kmod_ref, v_ref, tg_ref, g2_ref, rk_ref, lw_ref, lb_ref, o_ref):
    ones_bd = _head_sum_matrix()
    inv_n = 1.0 / RWKV_HEAD_SIZE
    gate = jnp.dot(tg_ref[...], g2_ref[...], preferred_element_type=F32)
    for c in range(y_ref.shape[1] // LANES):
        sl = slice(c * LANES, (c + 1) * LANES)
        y = y_ref[:, sl]
        mean = _head_sum(y, ones_bd) * inv_n
        yc = y - mean
        var = _head_sum(yc * yc, ones_bd) * inv_n
        yn = yc * lax.rsqrt(var + RWKV_GN_EPS) * lw_ref[:, sl] + lb_ref[:, sl]
        bonus = _head_sum(r_ref[:, sl].astype(F32) * kmod_ref[:, sl].astype(F32) * rk_ref[:, sl], ones_bd)
        o_ref[:, sl] = ((yn + bonus * v_ref[:, sl].astype(F32)) * gate[:, sl]).astype(o_ref.dtype)


def rwkv_layer(x, norm_g, mu, w_r, w_k, w_v, w_o, w0, w1, w2, a0, a1, a2, g1, g2,
               k_k, k_a, r_k, lnx_w, lnx_b, next_norm, batch, seq):
    m, d = x.shape
    tm = 256
    row_spec = pl.BlockSpec((tm, d), lambda i: (i, 0))
    vec_spec = pl.BlockSpec((1, d), lambda i: (0, 0))
    hb = tm // SUBLANES
    def lora_pad(w_in, w_out):
        rank = -(-w_in.shape[1] // LANES) * LANES
        return _pad_cols(w_in, rank).astype(BF16), _pad_rows(w_out, rank).astype(BF16)

    w1p, w2p = lora_pad(w1, w2)
    a1p, a2p = lora_pad(a1, a2)
    g1p, g2p = lora_pad(g1, g2)
    whole = lambda w: pl.BlockSpec(w.shape, lambda i: (0, 0))
    rows_of = lambda w: pl.BlockSpec((tm, w.shape[1]), lambda i: (i, 0))
    x_r, x_k, x_v, t_w, t_a, t_g = pl.pallas_call(
        functools.partial(_rwkv_mix_kernel, tiles_per_seq=seq // tm),
        grid=(m // tm,),
        in_specs=[row_spec, pl.BlockSpec((SUBLANES, d), lambda i: (jnp.maximum(i * hb - 1, 0), 0)),
                  vec_spec, pl.BlockSpec((6, d), lambda i: (0, 0)), whole(w1p), whole(a1p), whole(g1p)],
        out_specs=[row_spec] * 3 + [rows_of(w1p), rows_of(a1p), rows_of(g1p)],
        out_shape=[jax.ShapeDtypeStruct((m, d), BF16)] * 3
        + [jax.ShapeDtypeStruct((m, w.shape[1]), BF16) for w in (w1p, a1p, g1p)],
        compiler_params=_params("parallel"),
        name="rwkv_mix",
    )(x, x, norm_g.reshape(1, d), mu, w1p, a1p, g1p)
    r = matmul(x_r, w_r, out_dtype=BF16)
    k = matmul(x_k, w_k, out_dtype=BF16)
    v = matmul(x_v, w_v, out_dtype=BF16)

    def lora_specs(t, w):
        return (pl.BlockSpec((tm, t.shape[1]), lambda i: (i, 0)), pl.BlockSpec(w.shape, lambda i: (0, 0)))

    log_w, k_mod, a_neg, b_vec = pl.pallas_call(
        _rwkv_prep_kernel,
        grid=(m // tm,),
        in_specs=[row_spec, lora_specs(t_w, w2p)[0], lora_specs(t_a, a2p)[0], lora_specs(t_w, w2p)[1],
                  lora_specs(t_a, a2p)[1]] + [vec_spec] * 4,
        out_specs=[row_spec] * 4,
        out_shape=[jax.ShapeDtypeStruct((m, d), F32)] + [jax.ShapeDtypeStruct((m, d), BF16)] * 3,
        compiler_params=_params("parallel"),
        name="rwkv_prep",
    )(k, t_w, t_a, w2p, a2p, w0.reshape(1, d), a0.reshape(1, d), k_k.reshape(1, d),
      k_a.reshape(1, d))

    tb, groups_per_call = 256, 8
    width = groups_per_call * RWKV_GROUP * RWKV_HEAD_SIZE
    nt = seq // tb
    blk = pl.BlockSpec((tb, width), lambda b, p, t: (b * nt + t, p))
    y = pl.pallas_call(
        _rwkv_chunk_kernel,
        grid=(batch, d // width, nt),
        in_specs=[blk] * 6,
        out_specs=blk,
        out_shape=jax.ShapeDtypeStruct((m, d), F32),
        scratch_shapes=[pltpu.VMEM((groups_per_call, RWKV_GROUP * RWKV_HEAD_SIZE, RWKV_GROUP * RWKV_HEAD_SIZE), F32)],
        compiler_params=_params("parallel", "parallel", "arbitrary"),
        name="rwkv_chunks",
    )(r, log_w, k_mod, v, a_neg, b_vec)

    out = pl.pallas_call(
        _rwkv_post_kernel,
        grid=(m // tm,),
        in_specs=[row_spec] * 4 + list(lora_specs(t_g, g2p)) + [vec_spec] * 3,
        out_specs=row_spec,
        out_shape=jax.ShapeDtypeStruct((m, d), BF16),
        compiler_params=_params("parallel"),
        name="rwkv_post",
    )(y, r, k_mod, v, t_g, g2p, r_k.reshape(1, d), lnx_w.reshape(1, d), lnx_b.reshape(1, d))
    return matmul(out, w_o, residual=x, norm_gain=next_norm, tm=512)


def _softcap(z):
    return GATE_SOFTCAP * jnp.tanh(z / GATE_SOFTCAP)


def _mlstm_kernel(q_ref, k_ref, v_ref, o_ref, gc_ref, gr_ref, hg_ref, out_ref, ct_scr, n_scr, m_scr):
    L, H, dk, dv = MLSTM_CHUNK, MLSTM_HEADS, MLSTM_DQK, MLSTM_DV
    heads = range(H)

    @pl.when(pl.program_id(1) == 0)
    def _():
        ct_scr[...] = jnp.zeros_like(ct_scr)
        n_scr[...] = jnp.zeros_like(n_scr)
        m_scr[...] = jnp.zeros_like(m_scr)

    t_idx = lax.broadcasted_iota(jnp.int32, (L, L), 0)
    s_idx = lax.broadcasted_iota(jnp.int32, (L, L), 1)
    causal = s_idx <= t_idx
    anti = t_idx <= s_idx
    nt_dims = (((1,), (1,)), ((), ()))

    def chunk(c, carry):
        sl = pl.ds(pl.multiple_of(c * L, L), L)
        gc, gr = gc_ref[sl, :], gr_ref[c]
        li_cols, li_rows = _softcap(gc[:, :H]), _softcap(gr[:H, :])
        lf_cols = jax.nn.log_sigmoid(_softcap(gc[:, H:]))
        lf_rows = jax.nn.log_sigmoid(_softcap(gr[H:, :]))
        li_col = [li_cols[:, h:h + 1] for h in heads]
        li_row = [li_rows[h:h + 1, :] for h in heads]
        lf_col = [lf_cols[:, h:h + 1] for h in heads]
        lf_row = [lf_rows[h:h + 1, :] for h in heads]
        b_col = [jnp.sum(jnp.where(causal, lf_row[h], 0.0), axis=1, keepdims=True) for h in heads]
        b_row = [jnp.sum(jnp.where(anti, lf_col[h], 0.0), axis=0, keepdims=True) for h in heads]
        b_last = [jnp.sum(lf_row[h], axis=1, keepdims=True) for h in heads]
        m_prev = [m_scr[h] for h in heads]
        dmat = [jnp.where(causal, b_col[h] - b_row[h] + li_row[h], NEG_BIG) for h in heads]
        inter = [b_col[h] + m_prev[h] for h in heads]
        m_t = [jnp.maximum(inter[h], jnp.max(dmat[h], axis=1, keepdims=True)) for h in heads]
        q = [q_ref[sl, h * dk:(h + 1) * dk].astype(F32) for h in heads]
        k = [k_ref[sl, h * dk:(h + 1) * dk].astype(F32) * (dk ** -0.5) for h in heads]
        vb = [v_ref[sl, h * dv:(h + 1) * dv].astype(BF16) for h in heads]
        qb = [x.astype(BF16) for x in q]
        s = [lax.dot_general(qb[h], k[h].astype(BF16), nt_dims, preferred_element_type=F32) * jnp.exp(dmat[h] - m_t[h])
             for h in heads]
        w_inter = [jnp.exp(inter[h] - m_t[h]) for h in heads]
        ct = [ct_scr[h] for h in heads]
        n_row = [n_scr[h] for h in heads]
        num = [jnp.dot(s[h].astype(BF16), vb[h], preferred_element_type=F32)
               + w_inter[h] * jnp.dot(qb[h], ct[h].astype(BF16), preferred_element_type=F32) for h in heads]
        den = [jnp.sum(s[h], axis=1, keepdims=True) + w_inter[h] * jnp.sum(q[h] * n_row[h], axis=1, keepdims=True)
               for h in heads]
        h_c = [num[h] / jnp.maximum(jnp.abs(den[h]), jnp.exp(-m_t[h])) for h in heads]
        for h in heads:
            hn = h_c[h] * lax.rsqrt(jnp.mean(h_c[h] * h_c[h], axis=-1, keepdims=True) + NORM_EPS)
            cols = slice(h * dv, (h + 1) * dv)
            out_ref[sl, cols] = (hn * hg_ref[:, cols] * jax.nn.sigmoid(o_ref[sl, cols].astype(F32))).astype(out_ref.dtype)
        d_row = [b_last[h] - b_row[h] + li_row[h] for h in heads]
        m_new = [jnp.maximum(b_last[h] + m_prev[h], jnp.max(d_row[h], axis=1, keepdims=True)) for h in heads]
        kw = [k[h] * jnp.exp(b_last[h] - b_col[h] + li_col[h] - m_new[h]) for h in heads]
        w_c = [jnp.exp(b_last[h] + m_prev[h] - m_new[h]) for h in heads]
        for h in heads:
            ct_scr[h] = w_c[h] * ct[h] + jnp.dot(kw[h].T.astype(BF16), vb[h], preferred_element_type=F32)
            n_scr[h] = w_c[h] * n_row[h] + jnp.sum(kw[h], axis=0, keepdims=True)
            m_scr[h] = m_new[h]
        return carry

    lax.fori_loop(0, q_ref.shape[0] // L, chunk, 0)


def mlstm_layer(x, norm_g, w_in, b_if, head_gain, w_out, next_norm, batch, seq):
    m, d = x.shape
    H, L, dk, dv = MLSTM_HEADS, MLSTM_CHUNK, MLSTM_DQK, MLSTM_DV
    tb = 4 * L
    nt = seq // tb
    h = rmsnorm(x, norm_g)
    n_main = 2 * H * dk + 2 * H * dv
    proj = matmul(h, w_in, n=n_main, out_dtype=BF16)
    gates = matmul(h, _pad_cols(w_in[:, n_main:], LANES), bias=_pad_cols(b_if.reshape(1, 2 * H), LANES))
    gates = gates[:, :2 * H]
    gates_t = jnp.transpose(gates.reshape(m // L, L, 2 * H), (0, 2, 1))
    out = pl.pallas_call(
        _mlstm_kernel,
        grid=(batch, nt),
        in_specs=[pl.BlockSpec((tb, H * dk), lambda b, t: (b * nt + t, 0)),
                  pl.BlockSpec((tb, H * dk), lambda b, t: (b * nt + t, 1)),
                  pl.BlockSpec((tb, H * dv), lambda b, t: (b * nt + t, 1)),
                  pl.BlockSpec((tb, H * dv), lambda b, t: (b * nt + t, 2)),
                  pl.BlockSpec((tb, 2 * H), lambda b, t: (b * nt + t, 0)),
                  pl.BlockSpec((tb // L, 2 * H, L), lambda b, t: (b * nt + t, 0, 0)),
                  pl.BlockSpec((1, H * dv), lambda b, t: (0, 0))],
        out_specs=pl.BlockSpec((tb, H * dv), lambda b, t: (b * nt + t, 0)),
        out_shape=jax.ShapeDtypeStruct((m, H * dv), BF16),
        scratch_shapes=[pltpu.VMEM((H, dk, dv), F32), pltpu.VMEM((H, 1, dk), F32), pltpu.VMEM((H, 1, 1), F32)],
        compiler_params=_params("parallel", "arbitrary"),
        name="mlstm_chunks",
    )(proj, proj, proj, proj, gates, gates_t, head_gain.reshape(1, -1))
    return matmul(out, w_out, residual=x, norm_gain=next_norm, tm=512)


def kernel(x, moba_norm, moba_wqkv, moba_q_gain, moba_k_gain, moba_wo, rwkv_norm, rwkv_mu, rwkv_w_r, rwkv_w_k, rwkv_w_v, rwkv_w_o, rwkv_w0, rwkv_w1, rwkv_w2, rwkv_a0, rwkv_a1, rwkv_a2, rwkv_g1, rwkv_g2, rwkv_k_k, rwkv_k_a, rwkv_r_k, rwkv_lnx_w, rwkv_lnx_b, mlstm_norm, mlstm_w_in, mlstm_b_if, mlstm_head_gain, mlstm_w_out, ffn_norm, ffn_w_up, ffn_conv_w, ffn_conv_b, ffn_w_down):
    batch, seq, d = x.shape
    depth = ffn_norm.shape[0]
    x = x.reshape(batch * seq, d)
    for i in range(depth):
        kind, j = i % 3, i // 3
        if kind == 0:
            x, h = moba_layer(x, moba_norm[j], moba_wqkv, moba_q_gain[j], moba_k_gain[j], moba_wo, j,
                              ffn_norm[i], batch, seq)
        elif kind == 1:
            x, h = rwkv_layer(x, rwkv_norm[j], rwkv_mu[j], rwkv_w_r[j], rwkv_w_k[j], rwkv_w_v[j], rwkv_w_o[j],
                           rwkv_w0[j], rwkv_w1[j], rwkv_w2[j], rwkv_a0[j], rwkv_a1[j], rwkv_a2[j],
                           rwkv_g1[j], rwkv_g2[j], rwkv_k_k[j], rwkv_k_a[j], rwkv_r_k[j],
                           rwkv_lnx_w[j], rwkv_lnx_b[j], ffn_norm[i], batch, seq)
        else:
            x, h = mlstm_layer(x, mlstm_norm[j], mlstm_w_in[j], mlstm_b_if[j], mlstm_head_gain[j],
                               mlstm_w_out[j], ffn_norm[i], batch, seq)
        x = conv_ffn(x, h, ffn_w_up, ffn_conv_w, ffn_conv_b, ffn_w_down, i, seq)
    return x.reshape(batch, seq, d)
```
